```python
import math
import jax, jax.numpy as jnp
from jax import lax
import numpy as np

D_MODEL = 1024
BATCH = 32
SEQ = 2048
DEPTH = 2

RET_HEADS = 4
RET_QK_DIM = 64
RET_V_DIM = 128
RET_CHUNK = 128
DIFF_HEADS = 4
DIFF_HEAD_DIM = 64
DIFF_V_DIM = 2 * DIFF_HEAD_DIM
Q_BLOCK = 128
FFN_HIDDEN = ((int(math.ceil(8 * D_MODEL / 3)) + 255) // 256) * 256
NORM_EPS = 1e-6

RET_Q_W = RET_HEADS * RET_QK_DIM
RET_V_W = RET_HEADS * RET_V_DIM
DIFF_QK_W = DIFF_HEADS * 2 * DIFF_HEAD_DIM
DIFF_V_W = DIFF_HEADS * DIFF_V_DIM
IN_SPLITS = [RET_Q_W, RET_Q_W, RET_V_W, RET_V_W, DIFF_QK_W, DIFF_QK_W, DIFF_V_W]
IN_WIDTH = sum(IN_SPLITS)
MIX_WIDTH = RET_V_W + DIFF_V_W

kernel_name = "hybrid_retention_diffattn_alibi"


def rmsnorm(x, g):
    xf = x.astype(jnp.float32)
    y = xf * lax.rsqrt(jnp.mean(xf * xf, axis=-1, keepdims=True) + NORM_EPS)
    return (y * g.astype(jnp.float32)).astype(x.dtype)


def retention_gammas():
    return 1.0 - jnp.exp2(-5.0 - jnp.arange(RET_HEADS, dtype=jnp.float32))


def retention_chunkwise(q, k, v):
    B, S, H, dk = q.shape
    dv = v.shape[-1]
    C = RET_CHUNK
    N = S // C
    dt = q.dtype
    log_g = jnp.log(retention_gammas())
    k = k * jnp.asarray(dk ** -0.5, dt)
    q = q.reshape(B, N, C, H, dk)
    k = k.reshape(B, N, C, H, dk)
    v = v.reshape(B, N, C, H, dv)
    idx = jnp.arange(C, dtype=jnp.float32)
    rel = idx[:, None] - idx[None, :]
    decay = jnp.where(rel[None] >= 0,
                      jnp.exp(jnp.maximum(rel, 0.0)[None] * log_g[:, None, None]),
                      0.0).astype(dt)
    s = jnp.einsum('bnchd,bnshd->bnhcs', q, k) * decay
    inner = jnp.einsum('bnhcs,bnshe->bnche', s, v)
    k_decay = jnp.exp((C - 1 - idx)[:, None] * log_g[None, :]).astype(dt)
    kv = jnp.einsum('bnshd,bnshe->nbhde', k * k_decay[:, :, None], v)
    g_chunk = jnp.exp(C * log_g).astype(dt)[None, :, None, None]

    def step(state, kv_n):
        return g_chunk * state + kv_n, state

    _, states = lax.scan(step, jnp.zeros((B, H, dk, dv), dt), kv)
    q_decay = jnp.exp((idx + 1.0)[:, None] * log_g[None, :]).astype(dt)
    cross = jnp.einsum('bnchd,nbhde->bnche', q * q_decay[:, :, None], states)
    return (inner + cross).reshape(B, S, H, dv)


def alibi_slopes(n):
    return jnp.exp2(-8.0 * jnp.arange(1, n + 1, dtype=jnp.float32) / n)


def diff_attention(q, k, v, lam):
    S = q.shape[1]
    d = q.shape[-1]
    scale = d ** -0.5
    slopes = alibi_slopes(q.shape[2])
    outs = []
    for i in range(S // Q_BLOCK):
        q0 = i * Q_BLOCK
        kend = q0 + Q_BLOCK
        qb = q[:, q0:kend]
        kb = k[:, :kend]
        vb = v[:, :kend]
        s = jnp.einsum('bqhjd,bkhjd->bhjqk', qb, kb).astype(jnp.float32) * scale
        dist = (q0 + jnp.arange(Q_BLOCK, dtype=jnp.float32))[:, None] - jnp.arange(kend, dtype=jnp.float32)[None, :]
        s = s - slopes[None, :, None, None, None] * dist
        s = jnp.where(dist >= 0, s, -jnp.inf)
        p = jax.nn.softmax(s, axis=-1)
        a = p[:, :, 0] - lam * p[:, :, 1]
        outs.append(jnp.einsum('bhqk,bkhe->bqhe', a.astype(vb.dtype), vb))
    return jnp.concatenate(outs, axis=1)


def setup_inputs(seed: int = 0) -> dict:
    key = jax.random.key(seed)
    ks = jax.random.split(key, 16)
    f = jnp.float32

    def nrm(k, shape, scale):
        return jax.random.normal(k, shape, f) * scale

    def gain(k, shape):
        return 1.0 + 0.02 * jax.random.normal(k, shape, f)

    return {
        "x": jax.random.normal(ks[0], (BATCH, SEQ, D_MODEL), f),
        "attn_norm": gain(ks[1], (DEPTH, D_MODEL)),
        "w_in": nrm(ks[2], (DEPTH, D_MODEL, IN_WIDTH), D_MODEL ** -0.5),
        "ret_norm": gain(ks[3], (DEPTH, RET_V_DIM)),
        "lambda_q1": nrm(ks[4], (DEPTH, DIFF_HEAD_DIM), 0.1),
        "lambda_k1": nrm(ks[5], (DEPTH, DIFF_HEAD_DIM), 0.1),
        "lambda_q2": nrm(ks[6], (DEPTH, DIFF_HEAD_DIM), 0.1),
        "lambda_k2": nrm(ks[7], (DEPTH, DIFF_HEAD_DIM), 0.1),
        "diff_norm": gain(ks[8], (DEPTH, DIFF_V_DIM)),
        "w_out": nrm(ks[9], (DEPTH, MIX_WIDTH, D_MODEL), MIX_WIDTH ** -0.5),
        "ffn_norm": gain(ks[10], (DEPTH, D_MODEL)),
        "w_gate": nrm(ks[11], (DEPTH, D_MODEL, FFN_HIDDEN), D_MODEL ** -0.5),
        "w_up": nrm(ks[12], (DEPTH, D_MODEL, FFN_HIDDEN), D_MODEL ** -0.5),
        "w_down": nrm(ks[13], (DEPTH, FFN_HIDDEN, D_MODEL), FFN_HIDDEN ** -0.5),
        "final_norm": gain(ks[14], (D_MODEL,)),
    }


def reference(x, attn_norm, w_in, ret_norm, lambda_q1, lambda_k1, lambda_q2, lambda_k2,
              diff_norm, w_out, ffn_norm, w_gate, w_up, w_down, final_norm):
    B, S, _ = x.shape
    offsets = np.cumsum(IN_SPLITS)[:-1].tolist()
    h = x
    for l in range(DEPTH):
        u = rmsnorm(h, attn_norm[l])
        proj = u @ w_in[l]
        rq, rk, rv, rg, dq, dk, dv = jnp.split(proj, offsets, axis=-1)

        ret = retention_chunkwise(rq.reshape(B, S, RET_HEADS, RET_QK_DIM),
                                  rk.reshape(B, S, RET_HEADS, RET_QK_DIM),
                                  rv.reshape(B, S, RET_HEADS, RET_V_DIM))
        ret = rmsnorm(ret, ret_norm[l]).reshape(B, S, RET_V_W)
        ret = jax.nn.silu(rg) * ret

        lam_init = 0.8 - 0.6 * math.exp(-0.3 * l)
        lam = (jnp.exp(jnp.sum(lambda_q1[l].astype(jnp.float32) * lambda_k1[l].astype(jnp.float32)))
               - jnp.exp(jnp.sum(lambda_q2[l].astype(jnp.float32) * lambda_k2[l].astype(jnp.float32)))
               + lam_init)
        dif = diff_attention(dq.reshape(B, S, DIFF_HEADS, 2, DIFF_HEAD_DIM),
                             dk.reshape(B, S, DIFF_HEADS, 2, DIFF_HEAD_DIM),
                             dv.reshape(B, S, DIFF_HEADS, DIFF_V_DIM), lam)
        dif = (rmsnorm(dif, diff_norm[l]) * (1.0 - lam_init)).reshape(B, S, DIFF_V_W)

        mix = jnp.concatenate([ret, dif], axis=-1)
        h = h + mix @ w_out[l]

        u = rmsnorm(h, ffn_norm[l])
        h = h + (jax.nn.silu(u @ w_gate[l]) * (u @ w_up[l])) @ w_down[l]
    return rmsnorm(h, final_norm)
```

```python
import functools
import math

import numpy as np
import jax
import jax.numpy as jnp
from jax import lax
from jax.experimental import pallas as pl
from jax.experimental.pallas import tpu as pltpu

D_MODEL = 1024
RET_HEADS = 4
RET_QK_DIM = 64
RET_V_DIM = 128
RET_CHUNK = 128
DIFF_HEADS = 4
DIFF_HEAD_DIM = 64
DIFF_V_DIM = 2 * DIFF_HEAD_DIM
FFN_HIDDEN = 2816
NORM_EPS = 1e-6

RET_Q_W = RET_HEADS * RET_QK_DIM
RET_V_W = RET_HEADS * RET_V_DIM
DIFF_QK_W = DIFF_HEADS * 2 * DIFF_HEAD_DIM
DIFF_V_W = DIFF_HEADS * DIFF_V_DIM
IN_WIDTH = 2 * RET_Q_W + 2 * RET_V_W + 2 * DIFF_QK_W + DIFF_V_W
MIX_WIDTH = RET_V_W + DIFF_V_W

LANES = 128
_RQ_BLK256 = 0
_RK_BLK256 = 1
_RV_BLK512 = 1
_RG_BLK512 = 2
_DQ_BLK128 = 12
_DK_BLK128 = 16
_DV_BLK128 = 20

ROW_TILE = 512
ATTN_TILE = 256
VMEM_LIMIT = 56 * 1024 * 1024

F32 = jnp.float32
BF16 = jnp.bfloat16


def _rms(x, g):
    return x * lax.rsqrt(jnp.mean(x * x, axis=-1, keepdims=True) + NORM_EPS) * g


def _dot(a, b):
    return jnp.dot(a, b, preferred_element_type=F32)


def _dot_nt(a, b):
    return lax.dot_general(a, b, (((1,), (1,)), ((), ())), preferred_element_type=F32)


def _dot_tn(a, b):
    return lax.dot_general(a, b, (((0,), (0,)), ((), ())), preferred_element_type=F32)


def _resident(shape):
    return pl.BlockSpec(shape, lambda *_: (0,) * len(shape), pipeline_mode=pl.Buffered(1))


def _inproj_kernel(x_ref, g_ref, w_ref, o_ref):
    u = _rms(x_ref[...], g_ref[...]).astype(BF16)
    n = o_ref.shape[1]
    step = 1024
    for c0 in range(0, n, step):
        o_ref[:, c0:c0 + step] = _dot(u, w_ref[:, c0:c0 + step]).astype(BF16)


def _inproj(x2, gain, w):
    m = x2.shape[0]
    return pl.pallas_call(
        _inproj_kernel,
        out_shape=jax.ShapeDtypeStruct((m, IN_WIDTH), BF16),
        grid=(m // ROW_TILE,),
        in_specs=[
            pl.BlockSpec((ROW_TILE, D_MODEL), lambda i: (i, 0)),
            _resident((1, D_MODEL)),
            _resident((D_MODEL, IN_WIDTH)),
        ],
        out_specs=pl.BlockSpec((ROW_TILE, IN_WIDTH), lambda i: (i, 0)),
        compiler_params=pltpu.CompilerParams(
            dimension_semantics=("arbitrary",), vmem_limit_bytes=VMEM_LIMIT),
        name="inproj",
    )(x2, gain.reshape(1, D_MODEL), w)


def _retention_tables():
    c = RET_CHUNK
    gam = 1.0 - np.exp2(-5.0 - np.arange(RET_HEADS, dtype=np.float64))
    log_g = np.log(gam)
    idx = np.arange(c, dtype=np.float64)
    rel = idx[:, None] - idx[None, :]
    decay = np.where(rel[None] >= 0, np.exp(np.maximum(rel, 0.0)[None] * log_g[:, None, None]), 0.0)
    qdec = np.exp((idx + 1.0)[:, None] * log_g[None, :])
    kdec = np.exp((c - 1 - idx)[:, None] * log_g[None, :])
    qdec = np.repeat(qdec, RET_QK_DIM, axis=1)
    kdec = np.repeat(kdec, RET_QK_DIM, axis=1)
    gchunk = np.exp(c * log_g)
    return (decay.astype(np.float32), qdec.astype(np.float32), kdec.astype(np.float32),
            tuple(float(g) for g in gchunk))


def _retention_kernel(q_ref, k_ref, v_ref, g_ref, decay_ref, qdec_ref, kdec_ref, rn_ref,
                      o_ref, state_ref, *, gchunk):
    c = RET_CHUNK
    n_chunks = q_ref.shape[0] // c
    state_ref[...] = jnp.zeros_like(state_ref)
    lane = lax.broadcasted_iota(jnp.int32, (c, LANES), 1)
    low_half = lane < RET_QK_DIM
    rn = rn_ref[...]

    def chunk(n, _):
        r0 = pl.multiple_of(n * c, c)
        rows = pl.ds(r0, c)
        for pair in range(RET_HEADS // 2):
            cols = slice(pair * LANES, (pair + 1) * LANES)
            qp = q_ref[rows, cols]
            kp = k_ref[rows, cols] * jnp.asarray(RET_QK_DIM ** -0.5, BF16)
            kd = (kp.astype(F32) * kdec_ref[:, cols]).astype(BF16)
            for hh in range(2):
                h = 2 * pair + hh
                vcols = slice(h * RET_V_DIM, (h + 1) * RET_V_DIM)
                keep = low_half if hh == 0 else jnp.logical_not(low_half)
                qm = jnp.where(keep, qp, jnp.zeros_like(qp))
                vh = v_ref[rows, vcols]
                s = _dot_nt(qm, kp) * decay_ref[h]
                inner = _dot(s.astype(BF16), vh)
                qd = (qm.astype(F32) * qdec_ref[:, cols]).astype(BF16)
                st = state_ref[h]
                cross = _dot(qd, st.astype(BF16))
                state_ref[h] = gchunk[h] * st + _dot_tn(kd, vh)
                o = _rms(inner + cross, rn)
                gate = g_ref[rows, vcols].astype(F32)
                o = o * (gate * (1.0 / (1.0 + jnp.exp(-gate))))
                o_ref[rows, vcols] = o.astype(BF16)
        return 0

    lax.fori_loop(0, n_chunks, chunk, 0)


def _retention(proj3, ret_norm):
    b, s, _ = proj3.shape
    decay, qdec, kdec, gchunk = _retention_tables()
    c = RET_CHUNK
    return pl.pallas_call(
        functools.partial(_retention_kernel, gchunk=gchunk),
        out_shape=jax.ShapeDtypeStruct((b, s, RET_V_W), BF16),
        grid=(b,),
        in_specs=[
            pl.BlockSpec((None, s, RET_Q_W), lambda i: (i, 0, _RQ_BLK256)),
            pl.BlockSpec((None, s, RET_Q_W), lambda i: (i, 0, _RK_BLK256)),
            pl.BlockSpec((None, s, RET_V_W), lambda i: (i, 0, _RV_BLK512)),
            pl.BlockSpec((None, s, RET_V_W), lambda i: (i, 0, _RG_BLK512)),
            _resident((RET_HEADS, c, c)),
            _resident((c, RET_Q_W)),
            _resident((c, RET_Q_W)),
            _resident((1, RET_V_DIM)),
        ],
        out_specs=pl.BlockSpec((None, s, RET_V_W), lambda i: (i, 0, 0)),
        scratch_shapes=[pltpu.VMEM((RET_HEADS, LANES, RET_V_DIM), F32)],
        compiler_params=pltpu.CompilerParams(
            dimension_semantics=("arbitrary",), vmem_limit_bytes=VMEM_LIMIT),
        name="retention",
    )(proj3, proj3, proj3, proj3, jnp.asarray(decay), jnp.asarray(qdec), jnp.asarray(kdec),
      ret_norm.reshape(1, RET_V_DIM))


def _diffattn_kernel(slopes_ref, lq1_ref, lk1_ref, lq2_ref, lk2_ref, q_ref, k_ref, v_ref, dn_ref,
                     o_ref, *, lam_init):
    t = ATTN_TILE
    h = pl.program_id(1)
    slope = slopes_ref[h]
    n_q = q_ref.shape[0] // t
    lam = (jnp.exp(jnp.sum(lq1_ref[...] * lk1_ref[...], axis=-1, keepdims=True))
           - jnp.exp(jnp.sum(lq2_ref[...] * lk2_ref[...], axis=-1, keepdims=True)) + lam_init)
    dn = dn_ref[...] * (1.0 - lam_init)
    lane = lax.broadcasted_iota(jnp.int32, (t, LANES), 1)
    first = lane < DIFF_HEAD_DIM
    key_off = lax.broadcasted_iota(jnp.int32, (1, t), 1).astype(F32) * slope
    row = lax.broadcasted_iota(jnp.int32, (2 * t, t), 0)
    row = jnp.where(row >= t, row - t, row)
    col = lax.broadcasted_iota(jnp.int32, (2 * t, t), 1)
    causal = row >= col

    def qblock(qi, _):
        q0 = pl.multiple_of(qi * t, t)
        q = q_ref[pl.ds(q0, t), :]
        zero = jnp.zeros_like(q)
        qq = jnp.concatenate([jnp.where(first, q, zero), jnp.where(first, zero, q)], axis=0)
        qq = qq * jnp.asarray(DIFF_HEAD_DIM ** -0.5, BF16)

        s = _dot_nt(qq, k_ref[pl.ds(q0, t), :]) + key_off
        s = jnp.where(causal, s, -jnp.inf)
        m = jnp.max(s, axis=-1, keepdims=True)
        p = jnp.exp(s - m)
        l = jnp.sum(p, axis=-1, keepdims=True)
        acc = _dot(p.astype(BF16), v_ref[pl.ds(q0, t), :])

        def kvblock(j, carry):
            m, l, acc = carry
            k0 = pl.multiple_of(j * t, t)
            shift = (k0 - q0).astype(F32) * slope
            s = _dot_nt(qq, k_ref[pl.ds(k0, t), :]) + (key_off + shift)
            m_new = jnp.maximum(m, jnp.max(s, axis=-1, keepdims=True))
            alpha = jnp.exp(m - m_new)
            p = jnp.exp(s - m_new)
            l = alpha * l + jnp.sum(p, axis=-1, keepdims=True)
            acc = alpha * acc + _dot(p.astype(BF16), v_ref[pl.ds(k0, t), :])
            return m_new, l, acc

        m, l, acc = lax.fori_loop(0, qi, kvblock, (m, l, acc))
        a = acc * (1.0 / l)
        o = a[:t] - lam * a[t:]
        o_ref[pl.ds(q0, t), :] = _rms(o, dn).astype(BF16)
        return 0

    lax.fori_loop(0, n_q, qblock, 0)


def _diffattn(proj3, lq1, lk1, lq2, lk2, diff_norm, lam_init):
    b, s, _ = proj3.shape
    slopes = jnp.asarray(np.exp2(-8.0 * np.arange(1, DIFF_HEADS + 1) / DIFF_HEADS), F32)
    vec = lambda a: a.reshape(1, DIFF_HEAD_DIM).astype(F32)
    return pl.pallas_call(
        functools.partial(_diffattn_kernel, lam_init=lam_init),
        out_shape=jax.ShapeDtypeStruct((b, s, DIFF_V_W), BF16),
        grid=(b, DIFF_HEADS),
        in_specs=[
            pl.BlockSpec(memory_space=pltpu.SMEM),
            _resident((1, DIFF_HEAD_DIM)), _resident((1, DIFF_HEAD_DIM)),
            _resident((1, DIFF_HEAD_DIM)), _resident((1, DIFF_HEAD_DIM)),
            pl.BlockSpec((None, s, LANES), lambda i, h: (i, 0, _DQ_BLK128 + h)),
            pl.BlockSpec((None, s, LANES), lambda i, h: (i, 0, _DK_BLK128 + h)),
            pl.BlockSpec((None, s, LANES), lambda i, h: (i, 0, _DV_BLK128 + h)),
            _resident((1, DIFF_V_DIM)),
        ],
        out_specs=pl.BlockSpec((None, s, LANES), lambda i, h: (i, 0, h)),
        compiler_params=pltpu.CompilerParams(
            dimension_semantics=("arbitrary", "arbitrary"), vmem_limit_bytes=VMEM_LIMIT),
        name="diffattn",
    )(slopes, vec(lq1), vec(lk1), vec(lq2), vec(lk2), proj3, proj3, proj3,
      diff_norm.reshape(1, DIFF_V_DIM))


def _mix_ffn_kernel(h_ref, ret_ref, dif_ref, wo_ref, fn_ref, wg_ref, wu_ref, wd_ref, *rest, final):
    if final:
        final_ref, o_ref, a_ref = rest
    else:
        o_ref, a_ref = rest
    mix = jnp.concatenate([ret_ref[...], dif_ref[...]], axis=-1)
    h1 = h_ref[...] + _dot(mix, wo_ref[...])
    u = _rms(h1, fn_ref[...]).astype(BF16)
    step = 1024
    for c0 in range(0, FFN_HIDDEN, step):
        c1 = min(c0 + step, FFN_HIDDEN)
        g = _dot(u, wg_ref[:, c0:c1])
        up = _dot(u, wu_ref[:, c0:c1])
        a_ref[:, c0:c1] = (g * (1.0 / (1.0 + jnp.exp(-g))) * up).astype(BF16)
    out = h1 + _dot(a_ref[...], wd_ref[...])
    if final:
        out = _rms(out, final_ref[...])
    o_ref[...] = out


def _mix_ffn(h2, ret2, dif2, wo, fn, wg, wu, wd, final_norm):
    m = h2.shape[0]
    final = final_norm is not None
    row = lambda w: pl.BlockSpec((ROW_TILE, w), lambda i: (i, 0))
    in_specs = [
        row(D_MODEL), row(RET_V_W), row(DIFF_V_W),
        _resident((MIX_WIDTH, D_MODEL)),
        _resident((1, D_MODEL)),
        _resident((D_MODEL, FFN_HIDDEN)),
        _resident((D_MODEL, FFN_HIDDEN)),
        _resident((FFN_HIDDEN, D_MODEL)),
    ]
    args = [h2, ret2, dif2, wo, fn.reshape(1, D_MODEL), wg, wu, wd]
    if final:
        in_specs.append(_resident((1, D_MODEL)))
        args.append(final_norm.reshape(1, D_MODEL))
    return pl.pallas_call(
        functools.partial(_mix_ffn_kernel, final=final),
        out_shape=jax.ShapeDtypeStruct((m, D_MODEL), F32),
        grid=(m // ROW_TILE,),
        in_specs=in_specs,
        out_specs=row(D_MODEL),
        scratch_shapes=[pltpu.VMEM((ROW_TILE, FFN_HIDDEN), BF16)],
        compiler_params=pltpu.CompilerParams(
            dimension_semantics=("arbitrary",), vmem_limit_bytes=VMEM_LIMIT),
        name="mix_ffn_final" if final else "mix_ffn",
    )(*args)


def kernel(x, attn_norm, w_in, ret_norm, lambda_q1, lambda_k1, lambda_q2, lambda_k2,
           diff_norm, w_out, ffn_norm, w_gate, w_up, w_down, final_norm):
    b, s, _ = x.shape
    depth = w_in.shape[0]
    assert (b * s) % ROW_TILE == 0 and s % ATTN_TILE == 0 and s % RET_CHUNK == 0
    h = x.reshape(b * s, D_MODEL)
    for l in range(depth):
        proj = _inproj(h, attn_norm[l], w_in[l].astype(BF16))
        proj3 = proj.reshape(b, s, IN_WIDTH)
        ret = _retention(proj3, ret_norm[l])
        lam_init = 0.8 - 0.6 * math.exp(-0.3 * l)
        dif = _diffattn(proj3, lambda_q1[l], lambda_k1[l], lambda_q2[l], lambda_k2[l],
                        diff_norm[l], lam_init)
        h = _mix_ffn(h, ret.reshape(b * s, RET_V_W), dif.reshape(b * s, DIFF_V_W),
                     w_out[l].astype(BF16), ffn_norm[l], w_gate[l].astype(BF16),
                     w_up[l].astype(BF16), w_down[l].astype(BF16),
                     final_norm if l == depth - 1 else None)
    return h.reshape(b, s, D_MODEL)
```

```python
import functools
import math

import numpy as np
import jax
import jax.numpy as jnp
from jax import lax
from jax.experimental import pallas as pl
from jax.experimental.pallas import tpu as pltpu

D_MODEL = 1024
RET_HEADS = 4
RET_QK_DIM = 64
RET_V_DIM = 128
RET_CHUNK = 128
DIFF_HEADS = 4
DIFF_HEAD_DIM = 64
DIFF_V_DIM = 2 * DIFF_HEAD_DIM
FFN_HIDDEN = 2816
NORM_EPS = 1e-6

RET_Q_W = RET_HEADS * RET_QK_DIM
RET_V_W = RET_HEADS * RET_V_DIM
DIFF_QK_W = DIFF_HEADS * 2 * DIFF_HEAD_DIM
DIFF_V_W = DIFF_HEADS * DIFF_V_DIM
IN_WIDTH = 2 * RET_Q_W + 2 * RET_V_W + 2 * DIFF_QK_W + DIFF_V_W
MIX_WIDTH = RET_V_W + DIFF_V_W

LANES = 128
_RQ_BLK256 = 0
_RK_BLK256 = 1
_RV_BLK512 = 1
_RG_BLK512 = 2
_DQ_BLK512 = 3
_DK_BLK512 = 4
_DV_BLK512 = 5

ROW_TILE = 512
ATTN_TILE = 256
VMEM_LIMIT = 56 * 1024 * 1024

LOG2E = math.log2(math.e)
DQ_SCALE = DIFF_HEAD_DIM ** -0.5 * LOG2E

F32 = jnp.float32
BF16 = jnp.bfloat16


def _rms(x, g):
    return x * lax.rsqrt(jnp.mean(x * x, axis=-1, keepdims=True) + NORM_EPS) * g


def _dot(a, b):
    return jnp.dot(a, b, preferred_element_type=F32)


def _dot_nt(a, b):
    return lax.dot_general(a, b, (((1,), (1,)), ((), ())), preferred_element_type=F32)


def _dot_tn(a, b):
    return lax.dot_general(a, b, (((0,), (0,)), ((), ())), preferred_element_type=F32)


def _resident(shape):
    return pl.BlockSpec(shape, lambda *_: (0,) * len(shape), pipeline_mode=pl.Buffered(1))


def _inproj_kernel(x_ref, g_ref, w_ref, o_ref):
    u = _rms(x_ref[...], g_ref[...]).astype(BF16)
    step = DIFF_QK_W
    for blk in range(IN_WIDTH // step):
        cols = slice(blk * step, (blk + 1) * step)
        y = _dot(u, w_ref[:, cols])
        if blk == _DQ_BLK512:
            y = y * DQ_SCALE
        o_ref[:, cols] = y.astype(BF16)


def _inproj(x2, gain, w):
    m = x2.shape[0]
    return pl.pallas_call(
        _inproj_kernel,
        out_shape=jax.ShapeDtypeStruct((m, IN_WIDTH), BF16),
        grid=(m // ROW_TILE,),
        in_specs=[
            pl.BlockSpec((ROW_TILE, D_MODEL), lambda i: (i, 0)),
            _resident((1, D_MODEL)),
            _resident((D_MODEL, IN_WIDTH)),
        ],
        out_specs=pl.BlockSpec((ROW_TILE, IN_WIDTH), lambda i: (i, 0)),
        compiler_params=pltpu.CompilerParams(
            dimension_semantics=("arbitrary",), vmem_limit_bytes=VMEM_LIMIT),
        name="inproj",
    )(x2, gain.reshape(1, D_MODEL), w)


def _retention_tables():
    c = RET_CHUNK
    gam = 1.0 - np.exp2(-5.0 - np.arange(RET_HEADS, dtype=np.float64))
    log_g = np.log(gam)
    idx = np.arange(c, dtype=np.float64)
    rel = idx[:, None] - idx[None, :]
    decay = np.where(rel[None] >= 0, np.exp(np.maximum(rel, 0.0)[None] * log_g[:, None, None]), 0.0)
    qdec = np.exp((idx + 1.0)[:, None] * log_g[None, :])
    kdec = np.exp((c - 1 - idx)[:, None] * log_g[None, :])
    qdec = np.repeat(qdec, RET_QK_DIM, axis=1)
    kdec = np.repeat(kdec, RET_QK_DIM, axis=1)
    gchunk = np.exp(c * log_g)
    return (decay.astype(np.float32), qdec.astype(np.float32), kdec.astype(np.float32),
            tuple(float(g) for g in gchunk))


def _retention_kernel(q_ref, k_ref, v_ref, g_ref, decay_ref, qdec_ref, kdec_ref, rn_ref,
                      o_ref, state_ref, *, gchunk):
    c = RET_CHUNK
    n_chunks = q_ref.shape[0] // c
    state_ref[...] = jnp.zeros_like(state_ref)
    lane = lax.broadcasted_iota(jnp.int32, (c, LANES), 1)
    low_half = lane < RET_QK_DIM
    rn = rn_ref[...]

    def chunk(n, _):
        r0 = pl.multiple_of(n * c, c)
        rows = pl.ds(r0, c)
        for pair in range(RET_HEADS // 2):
            cols = slice(pair * LANES, (pair + 1) * LANES)
            qp = q_ref[rows, cols]
            kp = k_ref[rows, cols] * jnp.asarray(RET_QK_DIM ** -0.5, BF16)
            kd = (kp.astype(F32) * kdec_ref[:, cols]).astype(BF16)
            for hh in range(2):
                h = 2 * pair + hh
                vcols = slice(h * RET_V_DIM, (h + 1) * RET_V_DIM)
                keep = low_half if hh == 0 else jnp.logical_not(low_half)
                qm = jnp.where(keep, qp, jnp.zeros_like(qp))
                vh = v_ref[rows, vcols]
                s = _dot_nt(qm, kp) * decay_ref[h]
                inner = _dot(s.astype(BF16), vh)
                qd = (qm.astype(F32) * qdec_ref[:, cols]).astype(BF16)
                st = state_ref[h]
                cross = _dot(qd, st.astype(BF16))
                state_ref[h] = gchunk[h] * st + _dot_tn(kd, vh)
                o = _rms(inner + cross, rn)
                gate = g_ref[rows, vcols].astype(F32)
                o = o * (gate * (1.0 / (1.0 + jnp.exp(-gate))))
                o_ref[rows, vcols] = o.astype(BF16)
        return 0

    lax.fori_loop(0, n_chunks, chunk, 0)


def _retention(proj3, ret_norm):
    b, s, _ = proj3.shape
    decay, qdec, kdec, gchunk = _retention_tables()
    c = RET_CHUNK
    return pl.pallas_call(
        functools.partial(_retention_kernel, gchunk=gchunk),
        out_shape=jax.ShapeDtypeStruct((b, s, RET_V_W), BF16),
        grid=(b,),
        in_specs=[
            pl.BlockSpec((None, s, RET_Q_W), lambda i: (i, 0, _RQ_BLK256)),
            pl.BlockSpec((None, s, RET_Q_W), lambda i: (i, 0, _RK_BLK256)),
            pl.BlockSpec((None, s, RET_V_W), lambda i: (i, 0, _RV_BLK512)),
            pl.BlockSpec((None, s, RET_V_W), lambda i: (i, 0, _RG_BLK512)),
            _resident((RET_HEADS, c, c)),
            _resident((c, RET_Q_W)),
            _resident((c, RET_Q_W)),
            _resident((1, RET_V_DIM)),
        ],
        out_specs=pl.BlockSpec((None, s, RET_V_W), lambda i: (i, 0, 0)),
        scratch_shapes=[pltpu.VMEM((RET_HEADS, LANES, RET_V_DIM), F32)],
        compiler_params=pltpu.CompilerParams(
            dimension_semantics=("arbitrary",), vmem_limit_bytes=VMEM_LIMIT),
        name="retention",
    )(proj3, proj3, proj3, proj3, jnp.asarray(decay), jnp.asarray(qdec), jnp.asarray(kdec),
      ret_norm.reshape(1, RET_V_DIM))


def _diffattn_kernel(lq1_ref, lk1_ref, lq2_ref, lk2_ref, q_ref, k_ref, v_ref, dn_ref,
                     o_ref, bias_ref, qq_ref, m_ref, l_ref, acc_ref, *, lam_init):
    t = ATTN_TILE
    n_q = q_ref.shape[0] // t
    lam = (jnp.exp(jnp.sum(lq1_ref[...] * lk1_ref[...], axis=-1, keepdims=True))
           - jnp.exp(jnp.sum(lq2_ref[...] * lk2_ref[...], axis=-1, keepdims=True)) + lam_init)
    dn = dn_ref[...] * (1.0 - lam_init)
    lane = lax.broadcasted_iota(jnp.int32, (t, LANES), 1)
    first = lane < DIFF_HEAD_DIM
    slopes = [2.0 ** (-8.0 * (h + 1) / DIFF_HEADS) * LOG2E for h in range(DIFF_HEADS)]
    head_cols = [slice(h * LANES, (h + 1) * LANES) for h in range(DIFF_HEADS)]

    @pl.when(pl.program_id(0) == 0)
    def _():
        key = lax.broadcasted_iota(jnp.int32, (t, 2 * t), 0)
        qry = lax.broadcasted_iota(jnp.int32, (t, 2 * t), 1)
        causal = jnp.where(qry >= t, qry - t, qry) >= key
        for h in range(DIFF_HEADS):
            kb = key.astype(F32) * slopes[h]
            bias_ref[0, h] = kb
            bias_ref[1, h] = jnp.where(causal, kb, -jnp.inf)

    def qblock(qi, _):
        q0 = pl.multiple_of(qi * t, t)
        rows = pl.ds(q0, t)
        for h in range(DIFF_HEADS):
            q = q_ref[rows, head_cols[h]]
            zero = jnp.zeros_like(q)
            qq_ref[h] = jnp.concatenate([jnp.where(first, q, zero), jnp.where(first, zero, q)], axis=0)
        m_ref[...] = jnp.full_like(m_ref, -jnp.inf)
        l_ref[...] = jnp.zeros_like(l_ref)
        acc_ref[...] = jnp.zeros_like(acc_ref)

        def scores(j, h):
            k0 = pl.multiple_of(j * t, t)
            diag = (j == qi).astype(jnp.int32)
            s = _dot_nt(k_ref[pl.ds(k0, t), head_cols[h]], qq_ref[h]) + bias_ref[diag, h]
            return s, jnp.max(s, axis=0, keepdims=True)

        def update(j, h, s, s_max):
            k0 = pl.multiple_of(j * t, t)
            shift = (k0 - q0).astype(F32) * slopes[h]
            m_old = m_ref[h]
            m_new = jnp.maximum(m_old, s_max + shift)
            alpha = jnp.exp2(m_old - m_new)
            p = jnp.exp2(s - (m_new - shift))
            m_ref[h] = m_new
            l_ref[h] = alpha * l_ref[h] + jnp.sum(p, axis=0, keepdims=True)
            acc_ref[h] = alpha * acc_ref[h] + _dot_tn(v_ref[pl.ds(k0, t), head_cols[h]], p.astype(BF16))

        def kvblock(j, cur):
            for h in range(DIFF_HEADS):
                if h + 1 < DIFF_HEADS:
                    nxt = scores(j, h + 1)
                else:
                    nxt = scores(jnp.minimum(j + 1, qi), 0)
                update(j, h, *cur)
                cur = nxt
            return cur

        lax.fori_loop(0, qi + 1, kvblock, scores(0, 0))

        for h in range(DIFF_HEADS):
            a = acc_ref[h] * (1.0 / l_ref[h])
            o = a[:, :t] - lam * a[:, t:]
            o = o * lax.rsqrt(jnp.mean(o * o, axis=0, keepdims=True) + NORM_EPS) * dn
            o_ref[rows, head_cols[h]] = o.T.astype(BF16)
        return 0

    lax.fori_loop(0, n_q, qblock, 0)


def _diffattn(proj3, lq1, lk1, lq2, lk2, diff_norm, lam_init):
    b, s, _ = proj3.shape
    t = ATTN_TILE
    vec = lambda a: a.reshape(1, DIFF_HEAD_DIM).astype(F32)
    return pl.pallas_call(
        functools.partial(_diffattn_kernel, lam_init=lam_init),
        out_shape=jax.ShapeDtypeStruct((b, s, DIFF_V_W), BF16),
        grid=(b,),
        in_specs=[
            _resident((1, DIFF_HEAD_DIM)), _resident((1, DIFF_HEAD_DIM)),
            _resident((1, DIFF_HEAD_DIM)), _resident((1, DIFF_HEAD_DIM)),
            pl.BlockSpec((None, s, DIFF_QK_W), lambda i: (i, 0, _DQ_BLK512)),
            pl.BlockSpec((None, s, DIFF_QK_W), lambda i: (i, 0, _DK_BLK512)),
            pl.BlockSpec((None, s, DIFF_V_W), lambda i: (i, 0, _DV_BLK512)),
            _resident((DIFF_V_DIM, 1)),
        ],
        out_specs=pl.BlockSpec((None, s, DIFF_V_W), lambda i: (i, 0, 0)),
        scratch_shapes=[
            pltpu.VMEM((2, DIFF_HEADS, t, 2 * t), F32),
            pltpu.VMEM((DIFF_HEADS, 2 * t, LANES), BF16),
            pltpu.VMEM((DIFF_HEADS, 1, 2 * t), F32),
            pltpu.VMEM((DIFF_HEADS, 1, 2 * t), F32),
            pltpu.VMEM((DIFF_HEADS, DIFF_V_DIM, 2 * t), F32),
        ],
        compiler_params=pltpu.CompilerParams(
            dimension_semantics=("arbitrary",), vmem_limit_bytes=VMEM_LIMIT),
        name="diffattn",
    )(vec(lq1), vec(lk1), vec(lq2), vec(lk2), proj3, proj3, proj3,
      diff_norm.reshape(DIFF_V_DIM, 1))


def _mix_ffn_kernel(h_ref, ret_ref, dif_ref, wo_ref, fn_ref, wg_ref, wu_ref, wd_ref, *rest, final):
    if final:
        final_ref, o_ref, a_ref = rest
    else:
        o_ref, a_ref = rest
    mix = jnp.concatenate([ret_ref[...], dif_ref[...]], axis=-1)
    h1 = h_ref[...] + _dot(mix, wo_ref[...])
    u = _rms(h1, fn_ref[...]).astype(BF16)
    step = 1024
    for c0 in range(0, FFN_HIDDEN, step):
        c1 = min(c0 + step, FFN_HIDDEN)
        g = _dot(u, wg_ref[:, c0:c1])
        up = _dot(u, wu_ref[:, c0:c1])
        a_ref[:, c0:c1] = (g * (1.0 / (1.0 + jnp.exp(-g))) * up).astype(BF16)
    out = h1 + _dot(a_ref[...], wd_ref[...])
    if final:
        out = _rms(out, final_ref[...])
    o_ref[...] = out


def _mix_ffn(h2, ret2, dif2, wo, fn, wg, wu, wd, final_norm):
    m = h2.shape[0]
    final = final_norm is not None
    row = lambda w: pl.BlockSpec((ROW_TILE, w), lambda i: (i, 0))
    in_specs = [
        row(D_MODEL), row(RET_V_W), row(DIFF_V_W),
        _resident((MIX_WIDTH, D_MODEL)),
        _resident((1, D_MODEL)),
        _resident((D_MODEL, FFN_HIDDEN)),
        _resident((D_MODEL, FFN_HIDDEN)),
        _resident((FFN_HIDDEN, D_MODEL)),
    ]
    args = [h2, ret2, dif2, wo, fn.reshape(1, D_MODEL), wg, wu, wd]
    if final:
        in_specs.append(_resident((1, D_MODEL)))
        args.append(final_norm.reshape(1, D_MODEL))
    return pl.pallas_call(
        functools.partial(_mix_ffn_kernel, final=final),
        out_shape=jax.ShapeDtypeStruct((m, D_MODEL), F32),
        grid=(m // ROW_TILE,),
        in_specs=in_specs,
        out_specs=row(D_MODEL),
        scratch_shapes=[pltpu.VMEM((ROW_TILE, FFN_HIDDEN), BF16)],
        compiler_params=pltpu.CompilerParams(
            dimension_semantics=("arbitrary",), vmem_limit_bytes=VMEM_LIMIT),
        name="mix_ffn_final" if final else "mix_ffn",
    )(*args)


def kernel(x, attn_norm, w_in, ret_norm, lambda_q1, lambda_k1, lambda_q2, lambda_k2,
           diff_norm, w_out, ffn_norm, w_gate, w_up, w_down, final_norm):
    b, s, _ = x.shape
    depth = w_in.shape[0]
    assert (b * s) % ROW_TILE == 0 and s % ATTN_TILE == 0 and s % RET_CHUNK == 0
    h = x.reshape(b * s, D_MODEL)
    for l in range(depth):
        proj = _inproj(h, attn_norm[l], w_in[l].astype(BF16))
        proj3 = proj.reshape(b, s, IN_WIDTH)
        ret = _retention(proj3, ret_norm[l])
        lam_init = 0.8 - 0.6 * math.exp(-0.3 * l)
        dif = _diffattn(proj3, lambda_q1[l], lambda_k1[l], lambda_q2[l], lambda_k2[l],
                        diff_norm[l], lam_init)
        h = _mix_ffn(h, ret.reshape(b * s, RET_V_W), dif.reshape(b * s, DIFF_V_W),
                     w_out[l].astype(BF16), ffn_norm[l], w_gate[l].astype(BF16),
                     w_up[l].astype(BF16), w_down[l].astype(BF16),
                     final_norm if l == depth - 1 else None)
    return h.reshape(b, s, D_MODEL)
```

```python
import functools
import math

import numpy as np
import jax
import jax.numpy as jnp
from jax import lax
from jax.experimental import pallas as pl
from jax.experimental.pallas import tpu as pltpu

D_MODEL = 1024
RET_HEADS = 4
RET_QK_DIM = 64
RET_V_DIM = 128
RET_CHUNK = 128
DIFF_HEADS = 4
DIFF_HEAD_DIM = 64
DIFF_V_DIM = 2 * DIFF_HEAD_DIM
FFN_HIDDEN = 2816
NORM_EPS = 1e-6

RET_Q_W = RET_HEADS * RET_QK_DIM
RET_V_W = RET_HEADS * RET_V_DIM
DIFF_QK_W = DIFF_HEADS * 2 * DIFF_HEAD_DIM
DIFF_V_W = DIFF_HEADS * DIFF_V_DIM
IN_WIDTH = 2 * RET_Q_W + 2 * RET_V_W + 2 * DIFF_QK_W + DIFF_V_W
MIX_WIDTH = RET_V_W + DIFF_V_W

LANES = 128
_RQ_BLK256 = 0
_RK_BLK256 = 1
_RV_BLK512 = 1
_RG_BLK512 = 2
_DQ_BLK512 = 3
_DK_BLK512 = 4
_DV_BLK512 = 5

ROW_TILE = 512
ATTN_TILE = 256
VMEM_LIMIT = 56 * 1024 * 1024

LOG2E = math.log2(math.e)
DQ_SCALE = DIFF_HEAD_DIM ** -0.5 * LOG2E

F32 = jnp.float32
BF16 = jnp.bfloat16


def _rms(x, g):
    return x * lax.rsqrt(jnp.mean(x * x, axis=-1, keepdims=True) + NORM_EPS) * g


def _dot(a, b):
    return jnp.dot(a, b, preferred_element_type=F32)


def _dot_nt(a, b):
    return lax.dot_general(a, b, (((1,), (1,)), ((), ())), preferred_element_type=F32)


def _dot_tn(a, b):
    return lax.dot_general(a, b, (((0,), (0,)), ((), ())), preferred_element_type=F32)


def _resident(shape):
    return pl.BlockSpec(shape, lambda *_: (0,) * len(shape), pipeline_mode=pl.Buffered(1))


def _inproj_kernel(x_ref, g_ref, w_ref, o_ref):
    u = _rms(x_ref[...], g_ref[...]).astype(BF16)
    step = DIFF_QK_W
    for blk in range(IN_WIDTH // step):
        cols = slice(blk * step, (blk + 1) * step)
        y = _dot(u, w_ref[:, cols])
        if blk == _DQ_BLK512:
            y = y * DQ_SCALE
        o_ref[:, cols] = y.astype(BF16)


def _inproj(x2, gain, w):
    m = x2.shape[0]
    return pl.pallas_call(
        _inproj_kernel,
        out_shape=jax.ShapeDtypeStruct((m, IN_WIDTH), BF16),
        grid=(m // ROW_TILE,),
        in_specs=[
            pl.BlockSpec((ROW_TILE, D_MODEL), lambda i: (i, 0)),
            _resident((1, D_MODEL)),
            _resident((D_MODEL, IN_WIDTH)),
        ],
        out_specs=pl.BlockSpec((ROW_TILE, IN_WIDTH), lambda i: (i, 0)),
        compiler_params=pltpu.CompilerParams(
            dimension_semantics=("arbitrary",), vmem_limit_bytes=VMEM_LIMIT),
        name="inproj",
    )(x2, gain.reshape(1, D_MODEL), w)


def _retention_tables():
    c = RET_CHUNK
    gam = 1.0 - np.exp2(-5.0 - np.arange(RET_HEADS, dtype=np.float64))
    log_g = np.log(gam)
    idx = np.arange(c, dtype=np.float64)
    rel = idx[:, None] - idx[None, :]
    decay = np.where(rel[None] >= 0, np.exp(np.maximum(rel, 0.0)[None] * log_g[:, None, None]), 0.0)
    qdec = np.exp((idx + 1.0)[:, None] * log_g[None, :])
    kdec = np.exp((c - 1 - idx)[:, None] * log_g[None, :])
    qdec = np.repeat(qdec, RET_QK_DIM, axis=1)
    kdec = np.repeat(kdec, RET_QK_DIM, axis=1)
    gchunk = np.exp(c * log_g)
    return (decay.astype(np.float32), qdec.astype(np.float32), kdec.astype(np.float32),
            tuple(float(g) for g in gchunk))


def _retention_kernel(q_ref, k_ref, v_ref, g_ref, decay_ref, qdec_ref, kdec_ref, rn_ref,
                      o_ref, state_ref, *, gchunk):
    c = RET_CHUNK
    n_chunks = q_ref.shape[0] // c
    state_ref[...] = jnp.zeros_like(state_ref)
    lane = lax.broadcasted_iota(jnp.int32, (c, LANES), 1)
    low_half = lane < RET_QK_DIM
    rn = rn_ref[...]

    def chunk(n, _):
        r0 = pl.multiple_of(n * c, c)
        rows = pl.ds(r0, c)
        for pair in range(RET_HEADS // 2):
            cols = slice(pair * LANES, (pair + 1) * LANES)
            qp = q_ref[rows, cols]
            kp = k_ref[rows, cols] * jnp.asarray(RET_QK_DIM ** -0.5, BF16)
            kd = (kp.astype(F32) * kdec_ref[:, cols]).astype(BF16)
            for hh in range(2):
                h = 2 * pair + hh
                vcols = slice(h * RET_V_DIM, (h + 1) * RET_V_DIM)
                keep = low_half if hh == 0 else jnp.logical_not(low_half)
                qm = jnp.where(keep, qp, jnp.zeros_like(qp))
                vh = v_ref[rows, vcols]
                s = _dot_nt(qm, kp) * decay_ref[h]
                inner = _dot(s.astype(BF16), vh)
                qd = (qm.astype(F32) * qdec_ref[:, cols]).astype(BF16)
                st = state_ref[h]
                cross = _dot(qd, st.astype(BF16))
                state_ref[h] = gchunk[h] * st + _dot_tn(kd, vh)
                o = _rms(inner + cross, rn)
                gate = g_ref[rows, vcols].astype(F32)
                o = o * (gate * (1.0 / (1.0 + jnp.exp(-gate))))
                o_ref[rows, vcols] = o.astype(BF16)
        return 0

    lax.fori_loop(0, n_chunks, chunk, 0)


def _retention(proj3, ret_norm):
    b, s, _ = proj3.shape
    decay, qdec, kdec, gchunk = _retention_tables()
    c = RET_CHUNK
    return pl.pallas_call(
        functools.partial(_retention_kernel, gchunk=gchunk),
        out_shape=jax.ShapeDtypeStruct((b, s, RET_V_W), BF16),
        grid=(b,),
        in_specs=[
            pl.BlockSpec((None, s, RET_Q_W), lambda i: (i, 0, _RQ_BLK256)),
            pl.BlockSpec((None, s, RET_Q_W), lambda i: (i, 0, _RK_BLK256)),
            pl.BlockSpec((None, s, RET_V_W), lambda i: (i, 0, _RV_BLK512)),
            pl.BlockSpec((None, s, RET_V_W), lambda i: (i, 0, _RG_BLK512)),
            _resident((RET_HEADS, c, c)),
            _resident((c, RET_Q_W)),
            _resident((c, RET_Q_W)),
            _resident((1, RET_V_DIM)),
        ],
        out_specs=pl.BlockSpec((None, s, RET_V_W), lambda i: (i, 0, 0)),
        scratch_shapes=[pltpu.VMEM((RET_HEADS, LANES, RET_V_DIM), F32)],
        compiler_params=pltpu.CompilerParams(
            dimension_semantics=("arbitrary",), vmem_limit_bytes=VMEM_LIMIT),
        name="retention",
    )(proj3, proj3, proj3, proj3, jnp.asarray(decay), jnp.asarray(qdec), jnp.asarray(kdec),
      ret_norm.reshape(1, RET_V_DIM))


ATTN_AUG_ROWS = 16
ATTN_LOOKAHEAD = 2
_BIAS_TERMS = 3


def _bf16_pieces(x, n):
    out = []
    for _ in range(n):
        p = float(np.float32(x).astype(jnp.bfloat16))
        out.append(p)
        x -= p
    return out


def _steps_per_iter(n):
    return 4 if n % 4 == 0 else 2 if n % 2 == 0 else 1


def _diffattn_kernel(off_ref, lq1_ref, lk1_ref, lq2_ref, lk2_ref, q_ref, k_ref, v_ref, dn_ref,
                     o_ref, kaug_ref, qqt_ref, vt_ref, mask_ref, m_ref, acc_ref, *, lam_init, n_off):
    t = ATTN_TILE
    s_len = q_ref.shape[0]
    n_q = s_len // t
    d = DIFF_V_DIM
    lam = (jnp.exp(jnp.sum(lq1_ref[...] * lk1_ref[...], axis=-1, keepdims=True))
           - jnp.exp(jnp.sum(lq2_ref[...] * lk2_ref[...], axis=-1, keepdims=True)) + lam_init)
    dn = dn_ref[...] * (1.0 - lam_init)
    head_cols = [slice(h * LANES, (h + 1) * LANES) for h in range(DIFF_HEADS)]

    @pl.when(pl.program_id(0) == 0)
    def _():
        key = lax.broadcasted_iota(jnp.int32, (t, 2 * t), 0)
        qry = lax.broadcasted_iota(jnp.int32, (t, 2 * t), 1)
        causal = jnp.where(qry >= t, qry - t, qry) >= key
        mask_ref[...] = jnp.where(causal, 0.0, -jnp.inf)
        pos = lax.broadcasted_iota(jnp.int32, (s_len, LANES), 0)
        lane = lax.broadcasted_iota(jnp.int32, (s_len, LANES), 1)
        part = jnp.where(lane % 2 == 0, pos - pos % LANES, pos % LANES)
        part = jnp.where(lane < 2 * _BIAS_TERMS, part, 0).astype(F32).astype(BF16)
        row = lax.broadcasted_iota(jnp.int32, (LANES, 2 * t), 0)
        for h in range(DIFF_HEADS):
            kaug_ref[h, :, LANES:] = part
            slope = 2.0 ** (-8.0 * (h + 1) / DIFF_HEADS)
            coef = jnp.zeros((LANES, 2 * t), F32)
            for i, piece in enumerate(_bf16_pieces(LOG2E, _BIAS_TERMS)):
                coef = jnp.where(row // 2 == i, slope * piece, coef)
            for qb in range(n_q):
                qqt_ref[h, qb, LANES:, :] = coef.astype(BF16)
        vt_ref[:, :, d:, :] = jnp.ones((DIFF_HEADS, n_q, ATTN_AUG_ROWS, t), BF16)

    sub = lax.broadcasted_iota(jnp.int32, (LANES, t), 0)
    upper = sub < DIFF_HEAD_DIM

    def stage(jb, _):
        rows = pl.ds(pl.multiple_of(jb * t, t), t)
        for h in range(DIFF_HEADS):
            kaug_ref[h, rows, :LANES] = k_ref[rows, head_cols[h]]
            vt_ref[h, jb, :d, :] = v_ref[rows, head_cols[h]].T
            qt = q_ref[rows, head_cols[h]].T
            zero = jnp.zeros_like(qt)
            qqt_ref[h, jb, :LANES, :t] = jnp.where(upper, qt, zero)
            qqt_ref[h, jb, :LANES, t:] = jnp.where(upper, zero, qt)
        return 0

    lax.fori_loop(0, n_q, stage, 0)
    m_ref[...] = jnp.full_like(m_ref, -jnp.inf)
    acc_ref[...] = jnp.zeros_like(acc_ref)

    def run_steps(n_steps, pair_of, masked):
        per_iter = _steps_per_iter(n_steps)
        items = per_iter * DIFF_HEADS

        def scores(step, h):
            qi, j = pair_of(jnp.minimum(step, n_steps - 1))
            s = _dot(kaug_ref[h, pl.ds(pl.multiple_of(j * t, t), t), :], qqt_ref[h, qi])
            if masked:
                s = s + mask_ref[...]
            return s, jnp.max(s, axis=0, keepdims=True)

        def update(step, h, s, s_max):
            qi, j = pair_of(step)
            m_old = m_ref[h, qi]
            m_new = jnp.maximum(m_old, s_max)
            alpha = jnp.exp2(m_old - m_new)
            p = jnp.exp2(s - m_new).astype(BF16)
            m_ref[h, qi] = m_new
            acc_ref[h, qi] = alpha * acc_ref[h, qi] + _dot(vt_ref[h, j], p)

        def body(it, pending):
            pending = list(pending)
            base = it * per_iter
            for idx in range(items):
                ahead = idx + ATTN_LOOKAHEAD
                new = scores(base + ahead // DIFF_HEADS, ahead % DIFF_HEADS)
                update(base + idx // DIFF_HEADS, idx % DIFF_HEADS, *pending[0])
                pending = pending[1:] + [new]
            return tuple(pending)

        first = tuple(scores(i // DIFF_HEADS, i % DIFF_HEADS) for i in range(ATTN_LOOKAHEAD))
        lax.fori_loop(0, n_steps // per_iter, body, first)

    if n_off:
        run_steps(n_off, lambda step: (off_ref[0, step], off_ref[1, step]), masked=False)
    run_steps(n_q, lambda step: (step, step), masked=True)

    def finish(qb, _):
        rows = pl.ds(pl.multiple_of(qb * t, t), t)
        for h in range(DIFF_HEADS):
            acc = acc_ref[h, qb]
            a = acc[:d] * (1.0 / acc[d:d + 1])
            o = a[:, :t] - lam * a[:, t:]
            o = o * lax.rsqrt(jnp.mean(o * o, axis=0, keepdims=True) + NORM_EPS) * dn
            o_ref[rows, head_cols[h]] = o.T.astype(BF16)
        return 0

    lax.fori_loop(0, n_q, finish, 0)


def _diffattn(proj3, lq1, lk1, lq2, lk2, diff_norm, lam_init):
    b, s, _ = proj3.shape
    t = ATTN_TILE
    n_q = s // t
    off = np.array([(qi, j) for qi in range(n_q) for j in range(qi)], np.int32).reshape(-1, 2).T
    n_off = off.shape[1]
    if n_off == 0:
        off = np.zeros((2, 1), np.int32)
    vec = lambda a: a.reshape(1, DIFF_HEAD_DIM).astype(F32)
    rows_aug = DIFF_V_DIM + ATTN_AUG_ROWS
    return pl.pallas_call(
        functools.partial(_diffattn_kernel, lam_init=lam_init, n_off=n_off),
        out_shape=jax.ShapeDtypeStruct((b, s, DIFF_V_W), BF16),
        grid=(b,),
        in_specs=[
            pl.BlockSpec(memory_space=pltpu.SMEM),
            _resident((1, DIFF_HEAD_DIM)), _resident((1, DIFF_HEAD_DIM)),
            _resident((1, DIFF_HEAD_DIM)), _resident((1, DIFF_HEAD_DIM)),
            pl.BlockSpec((None, s, DIFF_QK_W), lambda i: (i, 0, _DQ_BLK512)),
            pl.BlockSpec((None, s, DIFF_QK_W), lambda i: (i, 0, _DK_BLK512)),
            pl.BlockSpec((None, s, DIFF_V_W), lambda i: (i, 0, _DV_BLK512)),
            _resident((DIFF_V_DIM, 1)),
        ],
        out_specs=pl.BlockSpec((None, s, DIFF_V_W), lambda i: (i, 0, 0)),
        scratch_shapes=[
            pltpu.VMEM((DIFF_HEADS, s, 2 * LANES), BF16),
            pltpu.VMEM((DIFF_HEADS, n_q, 2 * LANES, 2 * t), BF16),
            pltpu.VMEM((DIFF_HEADS, n_q, rows_aug, t), BF16),
            pltpu.VMEM((t, 2 * t), F32),
            pltpu.VMEM((DIFF_HEADS, n_q, 1, 2 * t), F32),
            pltpu.VMEM((DIFF_HEADS, n_q, rows_aug, 2 * t), F32),
        ],
        compiler_params=pltpu.CompilerParams(
            dimension_semantics=("arbitrary",), vmem_limit_bytes=VMEM_LIMIT),
        name="diffattn",
    )(jnp.asarray(off), vec(lq1), vec(lk1), vec(lq2), vec(lk2), proj3, proj3, proj3,
      diff_norm.reshape(DIFF_V_DIM, 1))


def _mix_ffn_kernel(h_ref, ret_ref, dif_ref, wo_ref, fn_ref, wg_ref, wu_ref, wd_ref, *rest, final):
    if final:
        final_ref, o_ref, a_ref = rest
    else:
        o_ref, a_ref = rest
    mix = jnp.concatenate([ret_ref[...], dif_ref[...]], axis=-1)
    h1 = h_ref[...] + _dot(mix, wo_ref[...])
    u = _rms(h1, fn_ref[...]).astype(BF16)
    step = 1024
    for c0 in range(0, FFN_HIDDEN, step):
        c1 = min(c0 + step, FFN_HIDDEN)
        g = _dot(u, wg_ref[:, c0:c1])
        up = _dot(u, wu_ref[:, c0:c1])
        a_ref[:, c0:c1] = (g * (1.0 / (1.0 + jnp.exp(-g))) * up).astype(BF16)
    out = h1 + _dot(a_ref[...], wd_ref[...])
    if final:
        out = _rms(out, final_ref[...])
    o_ref[...] = out


def _mix_ffn(h2, ret2, dif2, wo, fn, wg, wu, wd, final_norm):
    m = h2.shape[0]
    final = final_norm is not None
    row = lambda w: pl.BlockSpec((ROW_TILE, w), lambda i: (i, 0))
    in_specs = [
        row(D_MODEL), row(RET_V_W), row(DIFF_V_W),
        _resident((MIX_WIDTH, D_MODEL)),
        _resident((1, D_MODEL)),
        _resident((D_MODEL, FFN_HIDDEN)),
        _resident((D_MODEL, FFN_HIDDEN)),
        _resident((FFN_HIDDEN, D_MODEL)),
    ]
    args = [h2, ret2, dif2, wo, fn.reshape(1, D_MODEL), wg, wu, wd]
    if final:
        in_specs.append(_resident((1, D_MODEL)))
        args.append(final_norm.reshape(1, D_MODEL))
    return pl.pallas_call(
        functools.partial(_mix_ffn_kernel, final=final),
        out_shape=jax.ShapeDtypeStruct((m, D_MODEL), F32),
        grid=(m // ROW_TILE,),
        in_specs=in_specs,
        out_specs=row(D_MODEL),
        scratch_shapes=[pltpu.VMEM((ROW_TILE, FFN_HIDDEN), BF16)],
        compiler_params=pltpu.CompilerParams(
            dimension_semantics=("arbitrary",), vmem_limit_bytes=VMEM_LIMIT),
        name="mix_ffn_final" if final else "mix_ffn",
    )(*args)


def kernel(x, attn_norm, w_in, ret_norm, lambda_q1, lambda_k1, lambda_q2, lambda_k2,
           diff_norm, w_out, ffn_norm, w_gate, w_up, w_down, final_norm):
    b, s, _ = x.shape
    depth = w_in.shape[0]
    assert (b * s) % ROW_TILE == 0 and s % ATTN_TILE == 0 and s % RET_CHUNK == 0
    h = x.reshape(b * s, D_MODEL)
    for l in range(depth):
        proj = _inproj(h, attn_norm[l], w_in[l].astype(BF16))
        proj3 = proj.reshape(b, s, IN_WIDTH)
        ret = _retention(proj3, ret_norm[l])
        lam_init = 0.8 - 0.6 * math.exp(-0.3 * l)
        dif = _diffattn(proj3, lambda_q1[l], lambda_k1[l], lambda_q2[l], lambda_k2[l],
                        diff_norm[l], lam_init)
        h = _mix_ffn(h, ret.reshape(b * s, RET_V_W), dif.reshape(b * s, DIFF_V_W),
                     w_out[l].astype(BF16), ffn_norm[l], w_gate[l].astype(BF16),
                     w_up[l].astype(BF16), w_down[l].astype(BF16),
                     final_norm if l == depth - 1 else None)
    return h.reshape(b, s, D_MODEL)
```

```python
import functools
import math

import numpy as np
import jax
import jax.numpy as jnp
from jax import lax
from jax.experimental import pallas as pl
from jax.experimental.pallas import tpu as pltpu

D_MODEL = 1024
RET_HEADS = 4
RET_QK_DIM = 64
RET_V_DIM = 128
RET_CHUNK = 128
DIFF_HEADS = 4
DIFF_HEAD_DIM = 64
DIFF_V_DIM = 2 * DIFF_HEAD_DIM
FFN_HIDDEN = 2816
NORM_EPS = 1e-6

RET_Q_W = RET_HEADS * RET_QK_DIM
RET_V_W = RET_HEADS * RET_V_DIM
DIFF_QK_W = DIFF_HEADS * 2 * DIFF_HEAD_DIM
DIFF_V_W = DIFF_HEADS * DIFF_V_DIM
IN_WIDTH = 2 * RET_Q_W + 2 * RET_V_W + 2 * DIFF_QK_W + DIFF_V_W
MIX_WIDTH = RET_V_W + DIFF_V_W

LANES = 128
_RQ_BLK256 = 0
_RK_BLK256 = 1
_RV_BLK512 = 1
_RG_BLK512 = 2
_DQ_BLK512 = 3
_DK_BLK512 = 4
_DV_BLK512 = 5

ROW_TILE = 512
ATTN_TILE = 256
RET_CHUNKS_PER_ITER = 4
VMEM_LIMIT = 56 * 1024 * 1024

LOG2E = math.log2(math.e)
DQ_SCALE = DIFF_HEAD_DIM ** -0.5 * LOG2E

F32 = jnp.float32
BF16 = jnp.bfloat16


def _rms(x, g):
    return x * lax.rsqrt(jnp.mean(x * x, axis=-1, keepdims=True) + NORM_EPS) * g


def _dot(a, b):
    return jnp.dot(a, b, preferred_element_type=F32)


def _dot_nt(a, b):
    return lax.dot_general(a, b, (((1,), (1,)), ((), ())), preferred_element_type=F32)


def _dot_tn(a, b):
    return lax.dot_general(a, b, (((0,), (0,)), ((), ())), preferred_element_type=F32)


def _resident(shape):
    return pl.BlockSpec(shape, lambda *_: (0,) * len(shape), pipeline_mode=pl.Buffered(1))


def _inproj_kernel(x_ref, g_ref, w_ref, o_ref):
    u = _rms(x_ref[...], g_ref[...]).astype(BF16)
    step = DIFF_QK_W
    for blk in range(IN_WIDTH // step):
        cols = slice(blk * step, (blk + 1) * step)
        y = _dot(u, w_ref[:, cols])
        if blk == _DQ_BLK512:
            y = y * DQ_SCALE
        o_ref[:, cols] = y.astype(BF16)


def _inproj(x2, gain, w):
    m = x2.shape[0]
    return pl.pallas_call(
        _inproj_kernel,
        out_shape=jax.ShapeDtypeStruct((m, IN_WIDTH), BF16),
        grid=(m // ROW_TILE,),
        in_specs=[
            pl.BlockSpec((ROW_TILE, D_MODEL), lambda i: (i, 0)),
            _resident((1, D_MODEL)),
            _resident((D_MODEL, IN_WIDTH)),
        ],
        out_specs=pl.BlockSpec((ROW_TILE, IN_WIDTH), lambda i: (i, 0)),
        compiler_params=pltpu.CompilerParams(
            dimension_semantics=("arbitrary",), vmem_limit_bytes=VMEM_LIMIT),
        name="inproj",
    )(x2, gain.reshape(1, D_MODEL), w)


def _retention_tables():
    c = RET_CHUNK
    gam = 1.0 - np.exp2(-5.0 - np.arange(RET_HEADS, dtype=np.float64))
    log_g = np.log(gam)
    idx = np.arange(c, dtype=np.float64)
    rel = idx[:, None] - idx[None, :]
    decay = np.where(rel[None] >= 0, np.exp(np.maximum(rel, 0.0)[None] * log_g[:, None, None]), 0.0)
    qdec = np.exp((idx + 1.0)[:, None] * log_g[None, :])
    kdec = np.exp((c - 1 - idx)[:, None] * log_g[None, :])
    qdec = np.repeat(qdec, RET_QK_DIM, axis=1)
    kdec = np.repeat(kdec, RET_QK_DIM, axis=1)
    gchunk = np.exp(c * log_g)
    return (decay.astype(np.float32), qdec.astype(np.float32), kdec.astype(np.float32),
            tuple(float(g) for g in gchunk))


def _retention_kernel(q_ref, k_ref, v_ref, g_ref, decay_ref, qdec_ref, kdec_ref, rn_ref,
                      o_ref, state_ref, *, gchunk):
    c = RET_CHUNK
    n_chunks = q_ref.shape[0] // c
    n_pairs = RET_HEADS // 2
    state_ref[...] = jnp.zeros_like(state_ref)
    lane = lax.broadcasted_iota(jnp.int32, (c, LANES), 1)
    low_half = lane < RET_QK_DIM
    col = lax.broadcasted_iota(jnp.int32, (1, 2 * RET_V_DIM), 1)
    pair_decay = [jnp.where(col < RET_V_DIM, gchunk[2 * p], gchunk[2 * p + 1]) for p in range(n_pairs)]
    rn = rn_ref[...]

    per_iter = RET_CHUNKS_PER_ITER if n_chunks % RET_CHUNKS_PER_ITER == 0 else 1

    def chunks(it, _):
        rows = [pl.ds(pl.multiple_of((it * per_iter + i) * c, c), c) for i in range(per_iter)]
        scores, updates, qd_all = [], [], []
        for r in rows:
            q_all = q_ref[r, :]
            k_all = k_ref[r, :] * jnp.asarray(RET_QK_DIM ** -0.5, BF16)
            qd_all.append((q_all.astype(F32) * qdec_ref[...]).astype(BF16))
            kd_all = (k_all.astype(F32) * kdec_ref[...]).astype(BF16)
            for p in range(n_pairs):
                cols = slice(p * LANES, (p + 1) * LANES)
                qp = q_all[:, cols]
                zero = jnp.zeros_like(qp)
                qm2 = jnp.concatenate([jnp.where(low_half, qp, zero), jnp.where(low_half, zero, qp)], axis=0)
                scores.append(_dot_nt(qm2, k_all[:, cols]))
            for p in range(n_pairs):
                cols = slice(p * LANES, (p + 1) * LANES)
                vcols = slice(p * 2 * RET_V_DIM, (p + 1) * 2 * RET_V_DIM)
                updates.append(_dot_tn(kd_all[:, cols], v_ref[r, vcols]))
        state = [state_ref[p] for p in range(n_pairs)]
        for i, r in enumerate(rows):
            before = [st.astype(BF16) for st in state]
            state = [pair_decay[p] * state[p] + updates[i * n_pairs + p] for p in range(n_pairs)]
            for h in range(RET_HEADS):
                p, hh = divmod(h, 2)
                cols = slice(p * LANES, (p + 1) * LANES)
                vcols = slice(h * RET_V_DIM, (h + 1) * RET_V_DIM)
                sd = (scores[i * n_pairs + p][hh * c:(hh + 1) * c] * decay_ref[h]).astype(BF16)
                qd = qd_all[i][:, cols]
                keep = low_half if hh == 0 else jnp.logical_not(low_half)
                qd = jnp.where(keep, qd, jnp.zeros_like(qd))
                lhs = jnp.concatenate([sd, qd], axis=1)
                rhs = jnp.concatenate([v_ref[r, vcols], before[p][:, hh * RET_V_DIM:(hh + 1) * RET_V_DIM]],
                                      axis=0)
                o = _rms(_dot(lhs, rhs), rn)
                gate = g_ref[r, vcols].astype(F32)
                o = o * (gate * (1.0 / (1.0 + jnp.exp(-gate))))
                o_ref[r, vcols] = o.astype(BF16)
        for p in range(n_pairs):
            state_ref[p] = state[p]
        return 0

    lax.fori_loop(0, n_chunks // per_iter, chunks, 0)


def _retention(proj3, ret_norm):
    b, s, _ = proj3.shape
    decay, qdec, kdec, gchunk = _retention_tables()
    c = RET_CHUNK
    return pl.pallas_call(
        functools.partial(_retention_kernel, gchunk=gchunk),
        out_shape=jax.ShapeDtypeStruct((b, s, RET_V_W), BF16),
        grid=(b,),
        in_specs=[
            pl.BlockSpec((None, s, RET_Q_W), lambda i: (i, 0, _RQ_BLK256)),
            pl.BlockSpec((None, s, RET_Q_W), lambda i: (i, 0, _RK_BLK256)),
            pl.BlockSpec((None, s, RET_V_W), lambda i: (i, 0, _RV_BLK512)),
            pl.BlockSpec((None, s, RET_V_W), lambda i: (i, 0, _RG_BLK512)),
            _resident((RET_HEADS, c, c)),
            _resident((c, RET_Q_W)),
            _resident((c, RET_Q_W)),
            _resident((1, RET_V_DIM)),
        ],
        out_specs=pl.BlockSpec((None, s, RET_V_W), lambda i: (i, 0, 0)),
        scratch_shapes=[pltpu.VMEM((RET_HEADS // 2, LANES, 2 * RET_V_DIM), F32)],
        compiler_params=pltpu.CompilerParams(
            dimension_semantics=("arbitrary",), vmem_limit_bytes=VMEM_LIMIT),
        name="retention",
    )(proj3, proj3, proj3, proj3, jnp.asarray(decay), jnp.asarray(qdec), jnp.asarray(kdec),
      ret_norm.reshape(1, RET_V_DIM))


ATTN_AUG_ROWS = 16
ATTN_LOOKAHEAD = 2
_BIAS_TERMS = 3


def _bf16_pieces(x, n):
    out = []
    for _ in range(n):
        p = float(np.float32(x).astype(jnp.bfloat16))
        out.append(p)
        x -= p
    return out


def _steps_per_iter(n):
    return 4 if n % 4 == 0 else 2 if n % 2 == 0 else 1


def _diffattn_kernel(off_ref, lq1_ref, lk1_ref, lq2_ref, lk2_ref, q_ref, k_ref, v_ref, dn_ref,
                     o_ref, kaug_ref, qqt_ref, vt_ref, mask_ref, m_ref, acc_ref, *, lam_init, n_off):
    t = ATTN_TILE
    s_len = q_ref.shape[0]
    n_q = s_len // t
    d = DIFF_V_DIM
    lam = (jnp.exp(jnp.sum(lq1_ref[...] * lk1_ref[...], axis=-1, keepdims=True))
           - jnp.exp(jnp.sum(lq2_ref[...] * lk2_ref[...], axis=-1, keepdims=True)) + lam_init)
    dn = dn_ref[...] * (1.0 - lam_init)
    head_cols = [slice(h * LANES, (h + 1) * LANES) for h in range(DIFF_HEADS)]

    @pl.when(pl.program_id(0) == 0)
    def _():
        key = lax.broadcasted_iota(jnp.int32, (t, 2 * t), 0)
        qry = lax.broadcasted_iota(jnp.int32, (t, 2 * t), 1)
        causal = jnp.where(qry >= t, qry - t, qry) >= key
        mask_ref[...] = jnp.where(causal, 0.0, -jnp.inf)
        pos = lax.broadcasted_iota(jnp.int32, (s_len, LANES), 0)
        lane = lax.broadcasted_iota(jnp.int32, (s_len, LANES), 1)
        part = jnp.where(lane % 2 == 0, pos - pos % LANES, pos % LANES)
        part = jnp.where(lane < 2 * _BIAS_TERMS, part, 0).astype(F32).astype(BF16)
        row = lax.broadcasted_iota(jnp.int32, (LANES, 2 * t), 0)
        for h in range(DIFF_HEADS):
            kaug_ref[h, :, LANES:] = part
            slope = 2.0 ** (-8.0 * (h + 1) / DIFF_HEADS)
            coef = jnp.zeros((LANES, 2 * t), F32)
            for i, piece in enumerate(_bf16_pieces(LOG2E, _BIAS_TERMS)):
                coef = jnp.where(row // 2 == i, slope * piece, coef)
            for qb in range(n_q):
                qqt_ref[h, qb, LANES:, :] = coef.astype(BF16)
        vt_ref[:, :, d:, :] = jnp.ones((DIFF_HEADS, n_q, ATTN_AUG_ROWS, t), BF16)

    sub = lax.broadcasted_iota(jnp.int32, (LANES, t), 0)
    upper = sub < DIFF_HEAD_DIM

    def stage(jb, _):
        rows = pl.ds(pl.multiple_of(jb * t, t), t)
        for h in range(DIFF_HEADS):
            kaug_ref[h, rows, :LANES] = k_ref[rows, head_cols[h]]
            vt_ref[h, jb, :d, :] = v_ref[rows, head_cols[h]].T
            qt = q_ref[rows, head_cols[h]].T
            zero = jnp.zeros_like(qt)
            qqt_ref[h, jb, :LANES, :t] = jnp.where(upper, qt, zero)
            qqt_ref[h, jb, :LANES, t:] = jnp.where(upper, zero, qt)
        return 0

    lax.fori_loop(0, n_q, stage, 0)
    m_ref[...] = jnp.full_like(m_ref, -jnp.inf)
    acc_ref[...] = jnp.zeros_like(acc_ref)

    def run_steps(n_steps, pair_of, masked, final):
        per_iter = _steps_per_iter(n_steps)
        items = per_iter * DIFF_HEADS

        def scores(step, h):
            qi, j = pair_of(jnp.minimum(step, n_steps - 1))
            s = _dot(kaug_ref[h, pl.ds(pl.multiple_of(j * t, t), t), :], qqt_ref[h, qi])
            if masked:
                s = s + mask_ref[...]
            return s, jnp.max(s, axis=0, keepdims=True)

        def update(step, h, s, s_max):
            qi, j = pair_of(step)
            m_old = m_ref[h, qi]
            m_new = jnp.maximum(m_old, s_max)
            alpha = jnp.exp2(m_old - m_new)
            p = jnp.exp2(s - m_new).astype(BF16)
            m_ref[h, qi] = m_new
            acc = alpha * acc_ref[h, qi] + _dot(vt_ref[h, j], p)
            if final:
                finish(qi, h, acc)
            else:
                acc_ref[h, qi] = acc

        def body(it, pending):
            pending = list(pending)
            base = it * per_iter
            for idx in range(items):
                ahead = idx + ATTN_LOOKAHEAD
                new = scores(base + ahead // DIFF_HEADS, ahead % DIFF_HEADS)
                update(base + idx // DIFF_HEADS, idx % DIFF_HEADS, *pending[0])
                pending = pending[1:] + [new]
            return tuple(pending)

        first = tuple(scores(i // DIFF_HEADS, i % DIFF_HEADS) for i in range(ATTN_LOOKAHEAD))
        lax.fori_loop(0, n_steps // per_iter, body, first)

    def finish(qb, h, acc):
        a = acc[:d] * (1.0 / acc[d:d + 1])
        o = a[:, :t] - lam * a[:, t:]
        o = o * lax.rsqrt(jnp.mean(o * o, axis=0, keepdims=True) + NORM_EPS) * dn
        o_ref[pl.ds(pl.multiple_of(qb * t, t), t), head_cols[h]] = o.T.astype(BF16)

    if n_off:
        run_steps(n_off, lambda step: (off_ref[0, step], off_ref[1, step]), masked=False, final=False)
    run_steps(n_q, lambda step: (step, step), masked=True, final=True)


def _diffattn(proj3, lq1, lk1, lq2, lk2, diff_norm, lam_init):
    b, s, _ = proj3.shape
    t = ATTN_TILE
    n_q = s // t
    off = np.array([(qi, j) for qi in range(n_q) for j in range(qi)], np.int32).reshape(-1, 2).T
    n_off = off.shape[1]
    if n_off == 0:
        off = np.zeros((2, 1), np.int32)
    vec = lambda a: a.reshape(1, DIFF_HEAD_DIM).astype(F32)
    rows_aug = DIFF_V_DIM + ATTN_AUG_ROWS
    return pl.pallas_call(
        functools.partial(_diffattn_kernel, lam_init=lam_init, n_off=n_off),
        out_shape=jax.ShapeDtypeStruct((b, s, DIFF_V_W), BF16),
        grid=(b,),
        in_specs=[
            pl.BlockSpec(memory_space=pltpu.SMEM),
            _resident((1, DIFF_HEAD_DIM)), _resident((1, DIFF_HEAD_DIM)),
            _resident((1, DIFF_HEAD_DIM)), _resident((1, DIFF_HEAD_DIM)),
            pl.BlockSpec((None, s, DIFF_QK_W), lambda i: (i, 0, _DQ_BLK512)),
            pl.BlockSpec((None, s, DIFF_QK_W), lambda i: (i, 0, _DK_BLK512)),
            pl.BlockSpec((None, s, DIFF_V_W), lambda i: (i, 0, _DV_BLK512)),
            _resident((DIFF_V_DIM, 1)),
        ],
        out_specs=pl.BlockSpec((None, s, DIFF_V_W), lambda i: (i, 0, 0)),
        scratch_shapes=[
            pltpu.VMEM((DIFF_HEADS, s, 2 * LANES), BF16),
            pltpu.VMEM((DIFF_HEADS, n_q, 2 * LANES, 2 * t), BF16),
            pltpu.VMEM((DIFF_HEADS, n_q, rows_aug, t), BF16),
            pltpu.VMEM((t, 2 * t), F32),
            pltpu.VMEM((DIFF_HEADS, n_q, 1, 2 * t), F32),
            pltpu.VMEM((DIFF_HEADS, n_q, rows_aug, 2 * t), F32),
        ],
        compiler_params=pltpu.CompilerParams(
            dimension_semantics=("arbitrary",), vmem_limit_bytes=VMEM_LIMIT),
        name="diffattn",
    )(jnp.asarray(off), vec(lq1), vec(lk1), vec(lq2), vec(lk2), proj3, proj3, proj3,
      diff_norm.reshape(DIFF_V_DIM, 1))


def _mix_ffn_kernel(h_ref, ret_ref, dif_ref, wo_ref, fn_ref, wg_ref, wu_ref, wd_ref, *rest, final):
    if final:
        final_ref, o_ref, a_ref = rest
    else:
        o_ref, a_ref = rest
    mix = jnp.concatenate([ret_ref[...], dif_ref[...]], axis=-1)
    h1 = h_ref[...] + _dot(mix, wo_ref[...])
    u = _rms(h1, fn_ref[...]).astype(BF16)
    step = 1024
    for c0 in range(0, FFN_HIDDEN, step):
        c1 = min(c0 + step, FFN_HIDDEN)
        g = _dot(u, wg_ref[:, c0:c1])
        up = _dot(u, wu_ref[:, c0:c1])
        a_ref[:, c0:c1] = (g * (1.0 / (1.0 + jnp.exp(-g))) * up).astype(BF16)
    out = h1 + _dot(a_ref[...], wd_ref[...])
    if final:
        out = _rms(out, final_ref[...])
    o_ref[...] = out


def _mix_ffn(h2, ret2, dif2, wo, fn, wg, wu, wd, final_norm):
    m = h2.shape[0]
    final = final_norm is not None
    row = lambda w: pl.BlockSpec((ROW_TILE, w), lambda i: (i, 0))
    in_specs = [
        row(D_MODEL), row(RET_V_W), row(DIFF_V_W),
        _resident((MIX_WIDTH, D_MODEL)),
        _resident((1, D_MODEL)),
        _resident((D_MODEL, FFN_HIDDEN)),
        _resident((D_MODEL, FFN_HIDDEN)),
        _resident((FFN_HIDDEN, D_MODEL)),
    ]
    args = [h2, ret2, dif2, wo, fn.reshape(1, D_MODEL), wg, wu, wd]
    if final:
        in_specs.append(_resident((1, D_MODEL)))
        args.append(final_norm.reshape(1, D_MODEL))
    return pl.pallas_call(
        functools.partial(_mix_ffn_kernel, final=final),
        out_shape=jax.ShapeDtypeStruct((m, D_MODEL), F32),
        grid=(m // ROW_TILE,),
        in_specs=in_specs,
        out_specs=row(D_MODEL),
        scratch_shapes=[pltpu.VMEM((ROW_TILE, FFN_HIDDEN), BF16)],
        compiler_params=pltpu.CompilerParams(
            dimension_semantics=("arbitrary",), vmem_limit_bytes=VMEM_LIMIT),
        name="mix_ffn_final" if final else "mix_ffn",
    )(*args)


def kernel(x, attn_norm, w_in, ret_norm, lambda_q1, lambda_k1, lambda_q2, lambda_k2,
           diff_norm, w_out, ffn_norm, w_gate, w_up, w_down, final_norm):
    b, s, _ = x.shape
    depth = w_in.shape[0]
    assert (b * s) % ROW_TILE == 0 and s % ATTN_TILE == 0 and s % RET_CHUNK == 0
    h = x.reshape(b * s, D_MODEL)
    for l in range(depth):
        proj = _inproj(h, attn_norm[l], w_in[l].astype(BF16))
        proj3 = proj.reshape(b, s, IN_WIDTH)
        ret = _retention(proj3, ret_norm[l])
        lam_init = 0.8 - 0.6 * math.exp(-0.3 * l)
        dif = _diffattn(proj3, lambda_q1[l], lambda_k1[l], lambda_q2[l], lambda_k2[l],
                        diff_norm[l], lam_init)
        h = _mix_ffn(h, ret.reshape(b * s, RET_V_W), dif.reshape(b * s, DIFF_V_W),
                     w_out[l].astype(BF16), ffn_norm[l], w_gate[l].astype(BF16),
                     w_up[l].astype(BF16), w_down[l].astype(BF16),
                     final_norm if l == depth - 1 else None)
    return h.reshape(b, s, D_MODEL)
```

```python
import functools
import math

import numpy as np
import jax
import jax.numpy as jnp
from jax import lax
from jax.experimental import pallas as pl
from jax.experimental.pallas import tpu as pltpu

D_MODEL = 1024
RET_HEADS = 4
RET_QK_DIM = 64
RET_V_DIM = 128
RET_CHUNK = 128
DIFF_HEADS = 4
DIFF_HEAD_DIM = 64
DIFF_V_DIM = 2 * DIFF_HEAD_DIM
FFN_HIDDEN = 2816
NORM_EPS = 1e-6

RET_Q_W = RET_HEADS * RET_QK_DIM
RET_V_W = RET_HEADS * RET_V_DIM
DIFF_QK_W = DIFF_HEADS * 2 * DIFF_HEAD_DIM
DIFF_V_W = DIFF_HEADS * DIFF_V_DIM
IN_WIDTH = 2 * RET_Q_W + 2 * RET_V_W + 2 * DIFF_QK_W + DIFF_V_W
MIX_WIDTH = RET_V_W + DIFF_V_W

LANES = 128
_RQ_BLK256 = 0
_RK_BLK256 = 1
_RV_BLK512 = 1
_RG_BLK512 = 2
_DQ_BLK512 = 3
_DK_BLK512 = 4
_DV_BLK512 = 5

ROW_TILE = 512
ATTN_TILE = 256
RET_CHUNKS_PER_ITER = 4
VMEM_LIMIT = 56 * 1024 * 1024

LOG2E = math.log2(math.e)
DQ_SCALE = DIFF_HEAD_DIM ** -0.5 * LOG2E

F32 = jnp.float32
BF16 = jnp.bfloat16


def _rms(x, g):
    return x * lax.rsqrt(jnp.mean(x * x, axis=-1, keepdims=True) + NORM_EPS) * g


def _dot(a, b):
    return jnp.dot(a, b, preferred_element_type=F32)


def _dot_nt(a, b):
    return lax.dot_general(a, b, (((1,), (1,)), ((), ())), preferred_element_type=F32)


def _dot_tn(a, b):
    return lax.dot_general(a, b, (((0,), (0,)), ((), ())), preferred_element_type=F32)


def _resident(shape):
    return pl.BlockSpec(shape, lambda *_: (0,) * len(shape), pipeline_mode=pl.Buffered(1))


def _inproj_kernel(x_ref, g_ref, w_ref, o_ref):
    u = _rms(x_ref[...], g_ref[...]).astype(BF16)
    step = DIFF_QK_W
    for blk in range(IN_WIDTH // step):
        cols = slice(blk * step, (blk + 1) * step)
        y = _dot(u, w_ref[:, cols])
        if blk == _DQ_BLK512:
            y = y * DQ_SCALE
        o_ref[:, cols] = y.astype(BF16)


def _inproj(x2, gain, w):
    m = x2.shape[0]
    return pl.pallas_call(
        _inproj_kernel,
        out_shape=jax.ShapeDtypeStruct((m, IN_WIDTH), BF16),
        grid=(m // ROW_TILE,),
        in_specs=[
            pl.BlockSpec((ROW_TILE, D_MODEL), lambda i: (i, 0)),
            _resident((1, D_MODEL)),
            _resident((D_MODEL, IN_WIDTH)),
        ],
        out_specs=pl.BlockSpec((ROW_TILE, IN_WIDTH), lambda i: (i, 0)),
        compiler_params=pltpu.CompilerParams(
            dimension_semantics=("arbitrary",), vmem_limit_bytes=VMEM_LIMIT),
        name="inproj",
    )(x2, gain.reshape(1, D_MODEL), w)


def _retention_tables():
    c = RET_CHUNK
    gam = 1.0 - np.exp2(-5.0 - np.arange(RET_HEADS, dtype=np.float64))
    log_g = np.log(gam)
    idx = np.arange(c, dtype=np.float64)
    rel = idx[:, None] - idx[None, :]
    decay = np.where(rel[None] >= 0, np.exp(np.maximum(rel, 0.0)[None] * log_g[:, None, None]), 0.0)
    qdec = np.exp((idx + 1.0)[:, None] * log_g[None, :])
    kdec = np.exp((c - 1 - idx)[:, None] * log_g[None, :])
    qdec = np.repeat(qdec, RET_QK_DIM, axis=1)
    kdec = np.repeat(kdec, RET_QK_DIM, axis=1)
    gchunk = np.exp(c * log_g)
    return (decay.astype(np.float32), qdec.astype(np.float32), kdec.astype(np.float32),
            tuple(float(g) for g in gchunk))


def _retention_kernel(q_ref, k_ref, v_ref, g_ref, decay_ref, qdec_ref, kdec_ref, rn_ref,
                      o_ref, state_ref, *, gchunk):
    c = RET_CHUNK
    n_chunks = q_ref.shape[0] // c
    n_pairs = RET_HEADS // 2
    state_ref[...] = jnp.zeros_like(state_ref)
    lane = lax.broadcasted_iota(jnp.int32, (c, LANES), 1)
    low_half = lane < RET_QK_DIM
    col = lax.broadcasted_iota(jnp.int32, (1, 2 * RET_V_DIM), 1)
    pair_decay = [jnp.where(col < RET_V_DIM, gchunk[2 * p], gchunk[2 * p + 1]) for p in range(n_pairs)]
    rn = rn_ref[...]

    per_iter = RET_CHUNKS_PER_ITER if n_chunks % RET_CHUNKS_PER_ITER == 0 else 1

    def chunks(it, _):
        rows = [pl.ds(pl.multiple_of((it * per_iter + i) * c, c), c) for i in range(per_iter)]
        scores, updates, qd_all = [], [], []
        for r in rows:
            q_all = q_ref[r, :]
            k_all = k_ref[r, :] * jnp.asarray(RET_QK_DIM ** -0.5, BF16)
            qd_all.append((q_all.astype(F32) * qdec_ref[...]).astype(BF16))
            kd_all = (k_all.astype(F32) * kdec_ref[...]).astype(BF16)
            for p in range(n_pairs):
                cols = slice(p * LANES, (p + 1) * LANES)
                qp = q_all[:, cols]
                zero = jnp.zeros_like(qp)
                qm2 = jnp.concatenate([jnp.where(low_half, qp, zero), jnp.where(low_half, zero, qp)], axis=0)
                scores.append(_dot_nt(qm2, k_all[:, cols]))
            for p in range(n_pairs):
                cols = slice(p * LANES, (p + 1) * LANES)
                vcols = slice(p * 2 * RET_V_DIM, (p + 1) * 2 * RET_V_DIM)
                updates.append(_dot_tn(kd_all[:, cols], v_ref[r, vcols]))
        state = [state_ref[p] for p in range(n_pairs)]
        for i, r in enumerate(rows):
            before = [st.astype(BF16) for st in state]
            state = [pair_decay[p] * state[p] + updates[i * n_pairs + p] for p in range(n_pairs)]
            for h in range(RET_HEADS):
                p, hh = divmod(h, 2)
                cols = slice(p * LANES, (p + 1) * LANES)
                vcols = slice(h * RET_V_DIM, (h + 1) * RET_V_DIM)
                sd = (scores[i * n_pairs + p][hh * c:(hh + 1) * c] * decay_ref[h]).astype(BF16)
                qd = qd_all[i][:, cols]
                keep = low_half if hh == 0 else jnp.logical_not(low_half)
                qd = jnp.where(keep, qd, jnp.zeros_like(qd))
                lhs = jnp.concatenate([sd, qd], axis=1)
                rhs = jnp.concatenate([v_ref[r, vcols], before[p][:, hh * RET_V_DIM:(hh + 1) * RET_V_DIM]],
                                      axis=0)
                o = _rms(_dot(lhs, rhs), rn)
                gate = g_ref[r, vcols].astype(F32)
                o = o * (gate * (1.0 / (1.0 + jnp.exp(-gate))))
                o_ref[r, vcols] = o.astype(BF16)
        for p in range(n_pairs):
            state_ref[p] = state[p]
        return 0

    lax.fori_loop(0, n_chunks // per_iter, chunks, 0)


def _retention(proj3, ret_norm):
    b, s, _ = proj3.shape
    decay, qdec, kdec, gchunk = _retention_tables()
    c = RET_CHUNK
    return pl.pallas_call(
        functools.partial(_retention_kernel, gchunk=gchunk),
        out_shape=jax.ShapeDtypeStruct((b, s, RET_V_W), BF16),
        grid=(b,),
        in_specs=[
            pl.BlockSpec((None, s, RET_Q_W), lambda i: (i, 0, _RQ_BLK256)),
            pl.BlockSpec((None, s, RET_Q_W), lambda i: (i, 0, _RK_BLK256)),
            pl.BlockSpec((None, s, RET_V_W), lambda i: (i, 0, _RV_BLK512)),
            pl.BlockSpec((None, s, RET_V_W), lambda i: (i, 0, _RG_BLK512)),
            _resident((RET_HEADS, c, c)),
            _resident((c, RET_Q_W)),
            _resident((c, RET_Q_W)),
            _resident((1, RET_V_DIM)),
        ],
        out_specs=pl.BlockSpec((None, s, RET_V_W), lambda i: (i, 0, 0)),
        scratch_shapes=[pltpu.VMEM((RET_HEADS // 2, LANES, 2 * RET_V_DIM), F32)],
        compiler_params=pltpu.CompilerParams(
            dimension_semantics=("arbitrary",), vmem_limit_bytes=VMEM_LIMIT),
        name="retention",
    )(proj3, proj3, proj3, proj3, jnp.asarray(decay), jnp.asarray(qdec), jnp.asarray(kdec),
      ret_norm.reshape(1, RET_V_DIM))


ATTN_AUG_ROWS = 16
ATTN_RING = 4
ATTN_LOOKAHEAD = 2
_BIAS_TERMS = 3


def _bf16_pieces(x, n):
    out = []
    for _ in range(n):
        p = float(np.float32(x).astype(jnp.bfloat16))
        out.append(p)
        x -= p
    return out


def _steps_per_iter(n):
    return 4 if n % 4 == 0 else 2 if n % 2 == 0 else 1


def _diffattn_kernel(off_ref, lq1_ref, lk1_ref, lq2_ref, lk2_ref, q_ref, k_ref, v_ref, dn_ref,
                     o_ref, kaug_ref, qqt_ref, vt_ref, mask_ref, ring_ref, m_ref, acc_ref, *, lam_init, n_off):
    t = ATTN_TILE
    s_len = q_ref.shape[0]
    n_q = s_len // t
    d = DIFF_V_DIM
    lam = (jnp.exp(jnp.sum(lq1_ref[...] * lk1_ref[...], axis=-1, keepdims=True))
           - jnp.exp(jnp.sum(lq2_ref[...] * lk2_ref[...], axis=-1, keepdims=True)) + lam_init)
    dn = dn_ref[...] * (1.0 - lam_init)
    head_cols = [slice(h * LANES, (h + 1) * LANES) for h in range(DIFF_HEADS)]

    @pl.when(pl.program_id(0) == 0)
    def _():
        key = lax.broadcasted_iota(jnp.int32, (t, 2 * t), 0)
        qry = lax.broadcasted_iota(jnp.int32, (t, 2 * t), 1)
        causal = jnp.where(qry >= t, qry - t, qry) >= key
        mask_ref[...] = jnp.where(causal, 0.0, -jnp.inf)
        pos = lax.broadcasted_iota(jnp.int32, (s_len, LANES), 0)
        lane = lax.broadcasted_iota(jnp.int32, (s_len, LANES), 1)
        part = jnp.where(lane % 2 == 0, pos - pos % LANES, pos % LANES)
        part = jnp.where(lane < 2 * _BIAS_TERMS, part, 0).astype(F32).astype(BF16)
        row = lax.broadcasted_iota(jnp.int32, (LANES, 2 * t), 0)
        for h in range(DIFF_HEADS):
            kaug_ref[h, :, LANES:] = part
            slope = 2.0 ** (-8.0 * (h + 1) / DIFF_HEADS)
            coef = jnp.zeros((LANES, 2 * t), F32)
            for i, piece in enumerate(_bf16_pieces(LOG2E, _BIAS_TERMS)):
                coef = jnp.where(row // 2 == i, slope * piece, coef)
            for qb in range(n_q):
                qqt_ref[h, qb, LANES:, :] = coef.astype(BF16)
        vt_ref[:, :, d:, :] = jnp.ones((DIFF_HEADS, n_q, ATTN_AUG_ROWS, t), BF16)

    sub = lax.broadcasted_iota(jnp.int32, (LANES, t), 0)
    upper = sub < DIFF_HEAD_DIM

    def stage(jb, _):
        rows = pl.ds(pl.multiple_of(jb * t, t), t)
        for h in range(DIFF_HEADS):
            kaug_ref[h, rows, :LANES] = k_ref[rows, head_cols[h]]
            vt_ref[h, jb, :d, :] = v_ref[rows, head_cols[h]].T
            qt = q_ref[rows, head_cols[h]].T
            zero = jnp.zeros_like(qt)
            qqt_ref[h, jb, :LANES, :t] = jnp.where(upper, qt, zero)
            qqt_ref[h, jb, :LANES, t:] = jnp.where(upper, zero, qt)
            m_ref[h, jb] = jnp.full(m_ref.shape[2:], -jnp.inf, F32)
            acc_ref[h, jb] = jnp.zeros(acc_ref.shape[2:], F32)
        return 0

    lax.fori_loop(0, n_q, stage, 0)

    def run_steps(n_steps, pair_of, masked, final):
        per_iter = _steps_per_iter(n_steps)
        items = per_iter * DIFF_HEADS

        def scores(step, h, slot):
            qi, j = pair_of(jnp.minimum(step, n_steps - 1))
            s = _dot(kaug_ref[h, pl.ds(pl.multiple_of(j * t, t), t), :], qqt_ref[h, qi])
            if masked:
                s = s + mask_ref[...]
            ring_ref[slot] = s
            return jnp.max(s, axis=0, keepdims=True)

        def update(step, h, slot, s_max):
            qi, j = pair_of(step)
            s = ring_ref[slot]
            m_old = m_ref[h, qi]
            m_new = jnp.maximum(m_old, s_max)
            alpha = jnp.exp2(m_old - m_new)
            p = jnp.exp2(s - m_new).astype(BF16)
            m_ref[h, qi] = m_new
            acc = alpha * acc_ref[h, qi] + _dot(vt_ref[h, j], p)
            acc_ref[h, qi] = acc
            if final:
                finish(qi, h)

        def body(it, pending):
            pending = list(pending)
            base = it * per_iter
            for idx in range(items):
                ahead = idx + ATTN_LOOKAHEAD
                new = scores(base + ahead // DIFF_HEADS, ahead % DIFF_HEADS, ahead % ATTN_RING)
                update(base + idx // DIFF_HEADS, idx % DIFF_HEADS, idx % ATTN_RING, pending[0])
                pending = pending[1:] + [new]
            return tuple(pending)

        assert items % ATTN_RING == 0 and ATTN_LOOKAHEAD < ATTN_RING
        first = tuple(scores(i // DIFF_HEADS, i % DIFF_HEADS, i) for i in range(ATTN_LOOKAHEAD))
        lax.fori_loop(0, n_steps // per_iter, body, first)

    def finish(qb, h):
        inv = 1.0 / acc_ref[h, qb, d:d + 1, :]
        o = acc_ref[h, qb, :d, :t] * inv[:, :t] - acc_ref[h, qb, :d, t:] * (lam * inv[:, t:])
        o = o * lax.rsqrt(jnp.mean(o * o, axis=0, keepdims=True) + NORM_EPS) * dn
        o_ref[pl.ds(pl.multiple_of(qb * t, t), t), head_cols[h]] = o.T.astype(BF16)

    if n_off:
        run_steps(n_off, lambda step: (off_ref[0, step], off_ref[1, step]), masked=False, final=False)
    run_steps(n_q, lambda step: (step, step), masked=True, final=True)


def _diffattn(proj3, lq1, lk1, lq2, lk2, diff_norm, lam_init):
    b, s, _ = proj3.shape
    t = ATTN_TILE
    n_q = s // t
    off = np.array([(qi, j) for qi in range(n_q) for j in range(qi)], np.int32).reshape(-1, 2).T
    n_off = off.shape[1]
    if n_off == 0:
        off = np.zeros((2, 1), np.int32)
    vec = lambda a: a.reshape(1, DIFF_HEAD_DIM).astype(F32)
    rows_aug = DIFF_V_DIM + ATTN_AUG_ROWS
    return pl.pallas_call(
        functools.partial(_diffattn_kernel, lam_init=lam_init, n_off=n_off),
        out_shape=jax.ShapeDtypeStruct((b, s, DIFF_V_W), BF16),
        grid=(b,),
        in_specs=[
            pl.BlockSpec(memory_space=pltpu.SMEM),
            _resident((1, DIFF_HEAD_DIM)), _resident((1, DIFF_HEAD_DIM)),
            _resident((1, DIFF_HEAD_DIM)), _resident((1, DIFF_HEAD_DIM)),
            pl.BlockSpec((None, s, DIFF_QK_W), lambda i: (i, 0, _DQ_BLK512)),
            pl.BlockSpec((None, s, DIFF_QK_W), lambda i: (i, 0, _DK_BLK512)),
            pl.BlockSpec((None, s, DIFF_V_W), lambda i: (i, 0, _DV_BLK512)),
            _resident((DIFF_V_DIM, 1)),
        ],
        out_specs=pl.BlockSpec((None, s, DIFF_V_W), lambda i: (i, 0, 0)),
        scratch_shapes=[
            pltpu.VMEM((DIFF_HEADS, s, 2 * LANES), BF16),
            pltpu.VMEM((DIFF_HEADS, n_q, 2 * LANES, 2 * t), BF16),
            pltpu.VMEM((DIFF_HEADS, n_q, rows_aug, t), BF16),
            pltpu.VMEM((t, 2 * t), F32),
            pltpu.VMEM((ATTN_RING, t, 2 * t), F32),
            pltpu.VMEM((DIFF_HEADS, n_q, 1, 2 * t), F32),
            pltpu.VMEM((DIFF_HEADS, n_q, rows_aug, 2 * t), F32),
        ],
        compiler_params=pltpu.CompilerParams(
            dimension_semantics=("arbitrary",), vmem_limit_bytes=VMEM_LIMIT),
        name="diffattn",
    )(jnp.asarray(off), vec(lq1), vec(lk1), vec(lq2), vec(lk2), proj3, proj3, proj3,
      diff_norm.reshape(DIFF_V_DIM, 1))


def _mix_ffn_kernel(h_ref, ret_ref, dif_ref, wo_ref, fn_ref, wg_ref, wu_ref, wd_ref, *rest, final):
    if final:
        final_ref, o_ref, a_ref = rest
    else:
        o_ref, a_ref = rest
    mix = jnp.concatenate([ret_ref[...], dif_ref[...]], axis=-1)
    h1 = h_ref[...] + _dot(mix, wo_ref[...])
    u = _rms(h1, fn_ref[...]).astype(BF16)
    step = 1024
    for c0 in range(0, FFN_HIDDEN, step):
        c1 = min(c0 + step, FFN_HIDDEN)
        g = _dot(u, wg_ref[:, c0:c1])
        up = _dot(u, wu_ref[:, c0:c1])
        a_ref[:, c0:c1] = (g * (1.0 / (1.0 + jnp.exp(-g))) * up).astype(BF16)
    out = h1 + _dot(a_ref[...], wd_ref[...])
    if final:
        out = _rms(out, final_ref[...])
    o_ref[...] = out


def _mix_ffn(h2, ret2, dif2, wo, fn, wg, wu, wd, final_norm):
    m = h2.shape[0]
    final = final_norm is not None
    row = lambda w: pl.BlockSpec((ROW_TILE, w), lambda i: (i, 0))
    in_specs = [
        row(D_MODEL), row(RET_V_W), row(DIFF_V_W),
        _resident((MIX_WIDTH, D_MODEL)),
        _resident((1, D_MODEL)),
        _resident((D_MODEL, FFN_HIDDEN)),
        _resident((D_MODEL, FFN_HIDDEN)),
        _resident((FFN_HIDDEN, D_MODEL)),
    ]
    args = [h2, ret2, dif2, wo, fn.reshape(1, D_MODEL), wg, wu, wd]
    if final:
        in_specs.append(_resident((1, D_MODEL)))
        args.append(final_norm.reshape(1, D_MODEL))
    return pl.pallas_call(
        functools.partial(_mix_ffn_kernel, final=final),
        out_shape=jax.ShapeDtypeStruct((m, D_MODEL), F32),
        grid=(m // ROW_TILE,),
        in_specs=in_specs,
        out_specs=row(D_MODEL),
        scratch_shapes=[pltpu.VMEM((ROW_TILE, FFN_HIDDEN), BF16)],
        compiler_params=pltpu.CompilerParams(
            dimension_semantics=("arbitrary",), vmem_limit_bytes=VMEM_LIMIT),
        name="mix_ffn_final" if final else "mix_ffn",
    )(*args)


def kernel(x, attn_norm, w_in, ret_norm, lambda_q1, lambda_k1, lambda_q2, lambda_k2,
           diff_norm, w_out, ffn_norm, w_gate, w_up, w_down, final_norm):
    b, s, _ = x.shape
    depth = w_in.shape[0]
    assert (b * s) % ROW_TILE == 0 and s % ATTN_TILE == 0 and s % RET_CHUNK == 0
    h = x.reshape(b * s, D_MODEL)
    for l in range(depth):
        proj = _inproj(h, attn_norm[l], w_in[l].astype(BF16))
        proj3 = proj.reshape(b, s, IN_WIDTH)
        ret = _retention(proj3, ret_norm[l])
        lam_init = 0.8 - 0.6 * math.exp(-0.3 * l)
        dif = _diffattn(proj3, lambda_q1[l], lambda_k1[l], lambda_q2[l], lambda_k2[l],
                        diff_norm[l], lam_init)
        h = _mix_ffn(h, ret.reshape(b * s, RET_V_W), dif.reshape(b * s, DIFF_V_W),
                     w_out[l].astype(BF16), ffn_norm[l], w_gate[l].astype(BF16),
                     w_up[l].astype(BF16), w_down[l].astype(BF16),
                     final_norm if l == depth - 1 else None)
    return h.reshape(b, s, D_MODEL)
```

```python
import functools
import math

import numpy as np
import jax
import jax.numpy as jnp
from jax import lax
from jax.experimental import pallas as pl
from jax.experimental.pallas import tpu as pltpu

D_MODEL = 1024
RET_HEADS = 4
RET_QK_DIM = 64
RET_V_DIM = 128
RET_CHUNK = 128
DIFF_HEADS = 4
DIFF_HEAD_DIM = 64
DIFF_V_DIM = 2 * DIFF_HEAD_DIM
FFN_HIDDEN = 2816
NORM_EPS = 1e-6

RET_Q_W = RET_HEADS * RET_QK_DIM
RET_V_W = RET_HEADS * RET_V_DIM
DIFF_QK_W = DIFF_HEADS * 2 * DIFF_HEAD_DIM
DIFF_V_W = DIFF_HEADS * DIFF_V_DIM
IN_WIDTH = 2 * RET_Q_W + 2 * RET_V_W + 2 * DIFF_QK_W + DIFF_V_W
MIX_WIDTH = RET_V_W + DIFF_V_W

LANES = 128
_RQ_BLK256 = 0
_RK_BLK256 = 1
_RV_BLK512 = 1
_RG_BLK512 = 2
_DQ_BLK512 = 3
_DK_BLK512 = 4
_DV_BLK512 = 5

ROW_TILE = 512
ROW_SUBTILES = 2
ATTN_TILE = 256
RET_CHUNKS_PER_ITER = 4
VMEM_LIMIT = 56 * 1024 * 1024

LOG2E = math.log2(math.e)
DQ_SCALE = DIFF_HEAD_DIM ** -0.5 * LOG2E

F32 = jnp.float32
BF16 = jnp.bfloat16


def _rms(x, g):
    return x * lax.rsqrt(jnp.mean(x * x, axis=-1, keepdims=True) + NORM_EPS) * g


def _dot(a, b):
    return jnp.dot(a, b, preferred_element_type=F32)


def _dot_nt(a, b):
    return lax.dot_general(a, b, (((1,), (1,)), ((), ())), preferred_element_type=F32)


def _dot_tn(a, b):
    return lax.dot_general(a, b, (((0,), (0,)), ((), ())), preferred_element_type=F32)


def _resident(shape):
    return pl.BlockSpec(shape, lambda *_: (0,) * len(shape), pipeline_mode=pl.Buffered(1))


def _inproj_kernel(x_ref, g_ref, w_ref, o_ref):
    sub = ROW_TILE // ROW_SUBTILES
    u = [_rms(x_ref[i * sub:(i + 1) * sub, :], g_ref[...]).astype(BF16) for i in range(ROW_SUBTILES)]
    step = DIFF_QK_W
    for i in range(ROW_SUBTILES):
        rows = slice(i * sub, (i + 1) * sub)
        for blk in range(IN_WIDTH // step):
            cols = slice(blk * step, (blk + 1) * step)
            y = _dot(u[i], w_ref[:, cols])
            if blk == _DQ_BLK512:
                y = y * DQ_SCALE
            o_ref[rows, cols] = y.astype(BF16)


def _inproj(x2, gain, w):
    m = x2.shape[0]
    return pl.pallas_call(
        _inproj_kernel,
        out_shape=jax.ShapeDtypeStruct((m, IN_WIDTH), BF16),
        grid=(m // ROW_TILE,),
        in_specs=[
            pl.BlockSpec((ROW_TILE, D_MODEL), lambda i: (i, 0)),
            _resident((1, D_MODEL)),
            _resident((D_MODEL, IN_WIDTH)),
        ],
        out_specs=pl.BlockSpec((ROW_TILE, IN_WIDTH), lambda i: (i, 0)),
        compiler_params=pltpu.CompilerParams(
            dimension_semantics=("arbitrary",), vmem_limit_bytes=VMEM_LIMIT),
        name="inproj",
    )(x2, gain.reshape(1, D_MODEL), w)


def _retention_tables():
    c = RET_CHUNK
    gam = 1.0 - np.exp2(-5.0 - np.arange(RET_HEADS, dtype=np.float64))
    log_g = np.log(gam)
    idx = np.arange(c, dtype=np.float64)
    rel = idx[:, None] - idx[None, :]
    decay = np.where(rel[None] >= 0, np.exp(np.maximum(rel, 0.0)[None] * log_g[:, None, None]), 0.0)
    qdec = np.exp((idx + 1.0)[:, None] * log_g[None, :])
    kdec = np.exp((c - 1 - idx)[:, None] * log_g[None, :])
    qdec = np.repeat(qdec, RET_QK_DIM, axis=1)
    kdec = np.repeat(kdec, RET_QK_DIM, axis=1)
    gchunk = np.exp(c * log_g)
    return (decay.astype(np.float32), qdec.astype(np.float32), kdec.astype(np.float32),
            tuple(float(g) for g in gchunk))


def _retention_kernel(q_ref, k_ref, v_ref, decay_ref, qdec_ref, kdec_ref, o_ref, state_ref, *, gchunk):
    c = RET_CHUNK
    n_chunks = q_ref.shape[0] // c
    n_pairs = RET_HEADS // 2
    state_ref[...] = jnp.zeros_like(state_ref)
    lane = lax.broadcasted_iota(jnp.int32, (c, LANES), 1)
    low_half = lane < RET_QK_DIM
    col = lax.broadcasted_iota(jnp.int32, (1, 2 * RET_V_DIM), 1)
    pair_decay = [jnp.where(col < RET_V_DIM, gchunk[2 * p], gchunk[2 * p + 1]) for p in range(n_pairs)]

    per_iter = RET_CHUNKS_PER_ITER if n_chunks % RET_CHUNKS_PER_ITER == 0 else 1

    def chunks(it, _):
        rows = [pl.ds(pl.multiple_of((it * per_iter + i) * c, c), c) for i in range(per_iter)]
        scores, updates, qd_all = [], [], []
        for r in rows:
            q_all = q_ref[r, :]
            k_all = k_ref[r, :] * jnp.asarray(RET_QK_DIM ** -0.5, BF16)
            qd_all.append((q_all.astype(F32) * qdec_ref[...]).astype(BF16))
            kd_all = (k_all.astype(F32) * kdec_ref[...]).astype(BF16)
            for p in range(n_pairs):
                cols = slice(p * LANES, (p + 1) * LANES)
                qp = q_all[:, cols]
                zero = jnp.zeros_like(qp)
                qm2 = jnp.concatenate([jnp.where(low_half, qp, zero), jnp.where(low_half, zero, qp)], axis=0)
                scores.append(_dot_nt(qm2, k_all[:, cols]))
            for p in range(n_pairs):
                cols = slice(p * LANES, (p + 1) * LANES)
                vcols = slice(p * 2 * RET_V_DIM, (p + 1) * 2 * RET_V_DIM)
                updates.append(_dot_tn(kd_all[:, cols], v_ref[r, vcols]))
        state = [state_ref[p] for p in range(n_pairs)]
        for i, r in enumerate(rows):
            before = [st.astype(BF16) for st in state]
            state = [pair_decay[p] * state[p] + updates[i * n_pairs + p] for p in range(n_pairs)]
            for h in range(RET_HEADS):
                p, hh = divmod(h, 2)
                cols = slice(p * LANES, (p + 1) * LANES)
                vcols = slice(h * RET_V_DIM, (h + 1) * RET_V_DIM)
                sd = (scores[i * n_pairs + p][hh * c:(hh + 1) * c] * decay_ref[h]).astype(BF16)
                qd = qd_all[i][:, cols]
                keep = low_half if hh == 0 else jnp.logical_not(low_half)
                qd = jnp.where(keep, qd, jnp.zeros_like(qd))
                lhs = jnp.concatenate([sd, qd], axis=1)
                rhs = jnp.concatenate([v_ref[r, vcols], before[p][:, hh * RET_V_DIM:(hh + 1) * RET_V_DIM]],
                                      axis=0)
                o_ref[r, vcols] = _dot(lhs, rhs).astype(BF16)
        for p in range(n_pairs):
            state_ref[p] = state[p]
        return 0

    lax.fori_loop(0, n_chunks // per_iter, chunks, 0)


def _retention(proj3):
    b, s, _ = proj3.shape
    decay, qdec, kdec, gchunk = _retention_tables()
    c = RET_CHUNK
    return pl.pallas_call(
        functools.partial(_retention_kernel, gchunk=gchunk),
        out_shape=jax.ShapeDtypeStruct((b, s, RET_V_W), BF16),
        grid=(b,),
        in_specs=[
            pl.BlockSpec((None, s, RET_Q_W), lambda i: (i, 0, _RQ_BLK256)),
            pl.BlockSpec((None, s, RET_Q_W), lambda i: (i, 0, _RK_BLK256)),
            pl.BlockSpec((None, s, RET_V_W), lambda i: (i, 0, _RV_BLK512)),
            _resident((RET_HEADS, c, c)),
            _resident((c, RET_Q_W)),
            _resident((c, RET_Q_W)),
        ],
        out_specs=pl.BlockSpec((None, s, RET_V_W), lambda i: (i, 0, 0)),
        scratch_shapes=[pltpu.VMEM((RET_HEADS // 2, LANES, 2 * RET_V_DIM), F32)],
        compiler_params=pltpu.CompilerParams(
            dimension_semantics=("arbitrary",), vmem_limit_bytes=VMEM_LIMIT),
        name="retention",
    )(proj3, proj3, proj3, jnp.asarray(decay), jnp.asarray(qdec), jnp.asarray(kdec))


ATTN_AUG_ROWS = 16
ATTN_RING = 4
ATTN_LOOKAHEAD = 2
_BIAS_TERMS = 3


def _bf16_pieces(x, n):
    out = []
    for _ in range(n):
        p = float(np.float32(x).astype(jnp.bfloat16))
        out.append(p)
        x -= p
    return out


def _steps_per_iter(n):
    return next(k for k in (7, 4, 2, 1) if n % k == 0)


def _diffattn_kernel(off_ref, lq1_ref, lk1_ref, lq2_ref, lk2_ref, q_ref, k_ref, v_ref, dn_ref,
                     o_ref, kaug_ref, qqt_ref, vt_ref, mask_ref, ring_ref, m_ref, acc_ref, *, lam_init, n_off):
    t = ATTN_TILE
    s_len = q_ref.shape[0]
    n_q = s_len // t
    d = DIFF_V_DIM
    lam = (jnp.exp(jnp.sum(lq1_ref[...] * lk1_ref[...], axis=-1, keepdims=True))
           - jnp.exp(jnp.sum(lq2_ref[...] * lk2_ref[...], axis=-1, keepdims=True)) + lam_init)
    dn = dn_ref[...] * (1.0 - lam_init)
    head_cols = [slice(h * LANES, (h + 1) * LANES) for h in range(DIFF_HEADS)]

    @pl.when(pl.program_id(0) == 0)
    def _():
        key = lax.broadcasted_iota(jnp.int32, (t, 2 * t), 0)
        qry = lax.broadcasted_iota(jnp.int32, (t, 2 * t), 1)
        causal = jnp.where(qry >= t, qry - t, qry) >= key
        mask_ref[...] = jnp.where(causal, 0.0, -jnp.inf)
        pos = lax.broadcasted_iota(jnp.int32, (s_len, LANES), 0)
        lane = lax.broadcasted_iota(jnp.int32, (s_len, LANES), 1)
        part = jnp.where(lane % 2 == 0, pos - pos % LANES, pos % LANES)
        part = jnp.where(lane < 2 * _BIAS_TERMS, part, 0).astype(F32).astype(BF16)
        row = lax.broadcasted_iota(jnp.int32, (LANES, 2 * t), 0)
        for h in range(DIFF_HEADS):
            kaug_ref[h, :, LANES:] = part
            slope = 2.0 ** (-8.0 * (h + 1) / DIFF_HEADS)
            coef = jnp.zeros((LANES, 2 * t), F32)
            for i, piece in enumerate(_bf16_pieces(LOG2E, _BIAS_TERMS)):
                coef = jnp.where(row // 2 == i, slope * piece, coef)
            for qb in range(n_q):
                qqt_ref[h, qb, LANES:, :] = coef.astype(BF16)
        vt_ref[:, :, d:, :] = jnp.ones((DIFF_HEADS, n_q, ATTN_AUG_ROWS, t), BF16)

    sub = lax.broadcasted_iota(jnp.int32, (LANES, t), 0)
    upper = sub < DIFF_HEAD_DIM

    def stage(jb, _):
        rows = pl.ds(pl.multiple_of(jb * t, t), t)
        for h in range(DIFF_HEADS):
            kaug_ref[h, rows, :LANES] = k_ref[rows, head_cols[h]]
            vt_ref[h, jb, :d, :] = v_ref[rows, head_cols[h]].T
            qt = q_ref[rows, head_cols[h]].T
            zero = jnp.zeros_like(qt)
            qqt_ref[h, jb, :LANES, :t] = jnp.where(upper, qt, zero)
            qqt_ref[h, jb, :LANES, t:] = jnp.where(upper, zero, qt)
            m_ref[h, jb] = jnp.full(m_ref.shape[2:], -jnp.inf, F32)
            acc_ref[h, jb] = jnp.zeros(acc_ref.shape[2:], F32)
        return 0

    lax.fori_loop(0, n_q, stage, 0)

    def run_steps(n_steps, pair_of, masked, final):
        per_iter = _steps_per_iter(n_steps)
        items = per_iter * DIFF_HEADS

        def scores(step, h, slot):
            qi, j = pair_of(jnp.minimum(step, n_steps - 1))
            s = _dot(kaug_ref[h, pl.ds(pl.multiple_of(j * t, t), t), :], qqt_ref[h, qi])
            if masked:
                s = s + mask_ref[...]
            ring_ref[slot] = s
            return jnp.max(s, axis=0, keepdims=True)

        def update(step, h, slot, s_max):
            qi, j = pair_of(step)
            s = ring_ref[slot]
            m_old = m_ref[h, qi]
            m_new = jnp.maximum(m_old, s_max)
            alpha = jnp.exp2(m_old - m_new)
            p = jnp.exp2(s - m_new).astype(BF16)
            m_ref[h, qi] = m_new
            acc = alpha * acc_ref[h, qi] + _dot(vt_ref[h, j], p)
            acc_ref[h, qi] = acc
            if final:
                finish(qi, h)

        def body(it, pending):
            pending = list(pending)
            base = it * per_iter
            for idx in range(items):
                ahead = idx + ATTN_LOOKAHEAD
                new = scores(base + ahead // DIFF_HEADS, ahead % DIFF_HEADS, ahead % ATTN_RING)
                update(base + idx // DIFF_HEADS, idx % DIFF_HEADS, idx % ATTN_RING, pending[0])
                pending = pending[1:] + [new]
            return tuple(pending)

        assert items % ATTN_RING == 0 and ATTN_LOOKAHEAD < ATTN_RING
        first = tuple(scores(i // DIFF_HEADS, i % DIFF_HEADS, i) for i in range(ATTN_LOOKAHEAD))
        lax.fori_loop(0, n_steps // per_iter, body, first)

    def finish(qb, h):
        inv = 1.0 / acc_ref[h, qb, d:d + 1, :]
        o = acc_ref[h, qb, :d, :t] * inv[:, :t] - acc_ref[h, qb, :d, t:] * (lam * inv[:, t:])
        o = o * lax.rsqrt(jnp.mean(o * o, axis=0, keepdims=True) + NORM_EPS) * dn
        o_ref[pl.ds(pl.multiple_of(qb * t, t), t), head_cols[h]] = o.T.astype(BF16)

    if n_off:
        run_steps(n_off, lambda step: (off_ref[0, step], off_ref[1, step]), masked=False, final=False)
    run_steps(n_q, lambda step: (step, step), masked=True, final=True)


def _diffattn(proj3, lq1, lk1, lq2, lk2, diff_norm, lam_init):
    b, s, _ = proj3.shape
    t = ATTN_TILE
    n_q = s // t
    off = np.array([(qi, j) for qi in range(n_q) for j in range(qi)], np.int32).reshape(-1, 2).T
    n_off = off.shape[1]
    if n_off == 0:
        off = np.zeros((2, 1), np.int32)
    vec = lambda a: a.reshape(1, DIFF_HEAD_DIM).astype(F32)
    rows_aug = DIFF_V_DIM + ATTN_AUG_ROWS
    return pl.pallas_call(
        functools.partial(_diffattn_kernel, lam_init=lam_init, n_off=n_off),
        out_shape=jax.ShapeDtypeStruct((b, s, DIFF_V_W), BF16),
        grid=(b,),
        in_specs=[
            pl.BlockSpec(memory_space=pltpu.SMEM),
            _resident((1, DIFF_HEAD_DIM)), _resident((1, DIFF_HEAD_DIM)),
            _resident((1, DIFF_HEAD_DIM)), _resident((1, DIFF_HEAD_DIM)),
            pl.BlockSpec((None, s, DIFF_QK_W), lambda i: (i, 0, _DQ_BLK512)),
            pl.BlockSpec((None, s, DIFF_QK_W), lambda i: (i, 0, _DK_BLK512)),
            pl.BlockSpec((None, s, DIFF_V_W), lambda i: (i, 0, _DV_BLK512)),
            _resident((DIFF_V_DIM, 1)),
        ],
        out_specs=pl.BlockSpec((None, s, DIFF_V_W), lambda i: (i, 0, 0)),
        scratch_shapes=[
            pltpu.VMEM((DIFF_HEADS, s, 2 * LANES), BF16),
            pltpu.VMEM((DIFF_HEADS, n_q, 2 * LANES, 2 * t), BF16),
            pltpu.VMEM((DIFF_HEADS, n_q, rows_aug, t), BF16),
            pltpu.VMEM((t, 2 * t), F32),
            pltpu.VMEM((ATTN_RING, t, 2 * t), F32),
            pltpu.VMEM((DIFF_HEADS, n_q, 1, 2 * t), F32),
            pltpu.VMEM((DIFF_HEADS, n_q, rows_aug, 2 * t), F32),
        ],
        compiler_params=pltpu.CompilerParams(
            dimension_semantics=("arbitrary",), vmem_limit_bytes=VMEM_LIMIT),
        name="diffattn",
    )(jnp.asarray(off), vec(lq1), vec(lk1), vec(lq2), vec(lk2), proj3, proj3, proj3,
      diff_norm.reshape(DIFF_V_DIM, 1))


def _silu(x):
    return x * (1.0 / (1.0 + jnp.exp(-x)))


def _mix_ffn_kernel(h_ref, ret_ref, gate_ref, dif_ref, rn_ref, wo_ref, fn_ref, wg_ref, wu_ref, wd_ref,
                    *rest, final):
    if final:
        final_ref, o_ref, a_ref = rest
    else:
        o_ref, a_ref = rest
    sub = ROW_TILE // ROW_SUBTILES
    tiles = [slice(i * sub, (i + 1) * sub) for i in range(ROW_SUBTILES)]
    h1 = []
    for r in tiles:
        x = h_ref[r, :] + _dot(dif_ref[r, :], wo_ref[RET_V_W:, :])
        heads = []
        for hd in range(RET_HEADS):
            cols = slice(hd * RET_V_DIM, (hd + 1) * RET_V_DIM)
            ret = _rms(ret_ref[r, cols].astype(F32), rn_ref[...])
            heads.append((_silu(gate_ref[r, cols].astype(F32)) * ret).astype(BF16))
        h1.append(x + _dot(jnp.concatenate(heads, axis=-1), wo_ref[:RET_V_W, :]))
    u = [_rms(x, fn_ref[...]).astype(BF16) for x in h1]
    step = 1024
    for i, r in enumerate(tiles):
        for c0 in range(0, FFN_HIDDEN, step):
            c1 = min(c0 + step, FFN_HIDDEN)
            up = _dot(u[i], wu_ref[:, c0:c1])
            a_ref[r, c0:c1] = (_silu(_dot(u[i], wg_ref[:, c0:c1])) * up).astype(BF16)
        out = h1[i] + _dot(a_ref[r, :], wd_ref[...])
        if final:
            out = _rms(out, final_ref[...])
        o_ref[r, :] = out


def _mix_ffn(h2, ret2, proj, dif2, rn, wo, fn, wg, wu, wd, final_norm):
    m = h2.shape[0]
    final = final_norm is not None
    row = lambda w, blk=0: pl.BlockSpec((ROW_TILE, w), lambda i: (i, blk))
    in_specs = [
        row(D_MODEL), row(RET_V_W), row(RET_V_W, _RG_BLK512), row(DIFF_V_W),
        _resident((1, RET_V_DIM)),
        _resident((MIX_WIDTH, D_MODEL)),
        _resident((1, D_MODEL)),
        _resident((D_MODEL, FFN_HIDDEN)),
        _resident((D_MODEL, FFN_HIDDEN)),
        _resident((FFN_HIDDEN, D_MODEL)),
    ]
    args = [h2, ret2, proj, dif2, rn.reshape(1, RET_V_DIM), wo, fn.reshape(1, D_MODEL), wg, wu, wd]
    if final:
        in_specs.append(_resident((1, D_MODEL)))
        args.append(final_norm.reshape(1, D_MODEL))
    return pl.pallas_call(
        functools.partial(_mix_ffn_kernel, final=final),
        out_shape=jax.ShapeDtypeStruct((m, D_MODEL), F32),
        grid=(m // ROW_TILE,),
        in_specs=in_specs,
        out_specs=row(D_MODEL),
        scratch_shapes=[pltpu.VMEM((ROW_TILE, FFN_HIDDEN), BF16)],
        compiler_params=pltpu.CompilerParams(
            dimension_semantics=("arbitrary",), vmem_limit_bytes=VMEM_LIMIT),
        name="mix_ffn_final" if final else "mix_ffn",
    )(*args)


def kernel(x, attn_norm, w_in, ret_norm, lambda_q1, lambda_k1, lambda_q2, lambda_k2,
           diff_norm, w_out, ffn_norm, w_gate, w_up, w_down, final_norm):
    b, s, _ = x.shape
    depth = w_in.shape[0]
    assert (b * s) % ROW_TILE == 0 and s % ATTN_TILE == 0 and s % RET_CHUNK == 0
    h = x.reshape(b * s, D_MODEL)
    for l in range(depth):
        proj = _inproj(h, attn_norm[l], w_in[l].astype(BF16))
        proj3 = proj.reshape(b, s, IN_WIDTH)
        ret = _retention(proj3)
        lam_init = 0.8 - 0.6 * math.exp(-0.3 * l)
        dif = _diffattn(proj3, lambda_q1[l], lambda_k1[l], lambda_q2[l], lambda_k2[l],
                        diff_norm[l], lam_init)
        h = _mix_ffn(h, ret.reshape(b * s, RET_V_W), proj, dif.reshape(b * s, DIFF_V_W), ret_norm[l],
                     w_out[l].astype(BF16), ffn_norm[l], w_gate[l].astype(BF16),
                     w_up[l].astype(BF16), w_down[l].astype(BF16),
                     final_norm if l == depth - 1 else None)
    return h.reshape(b, s, D_MODEL)
```

```python
import functools
import math

import numpy as np
import jax
import jax.numpy as jnp
from jax import lax
from jax.experimental import pallas as pl
from jax.experimental.pallas import tpu as pltpu

D_MODEL = 1024
RET_HEADS = 4
RET_QK_DIM = 64
RET_V_DIM = 128
RET_CHUNK = 128
DIFF_HEADS = 4
DIFF_HEAD_DIM = 64
DIFF_V_DIM = 2 * DIFF_HEAD_DIM
FFN_HIDDEN = 2816
NORM_EPS = 1e-6

RET_Q_W = RET_HEADS * RET_QK_DIM
RET_V_W = RET_HEADS * RET_V_DIM
DIFF_QK_W = DIFF_HEADS * 2 * DIFF_HEAD_DIM
DIFF_V_W = DIFF_HEADS * DIFF_V_DIM
IN_WIDTH = 2 * RET_Q_W + 2 * RET_V_W + 2 * DIFF_QK_W + DIFF_V_W
MIX_WIDTH = RET_V_W + DIFF_V_W

LANES = 128
_RQ_BLK256 = 0
_RK_BLK256 = 1
_RV_BLK512 = 1
_RG_BLK512 = 2
_DQ_BLK512 = 3
_DK_BLK512 = 4
_DV_BLK512 = 5

ROW_TILE = 1024
ROW_SUBTILES = 4
ATTN_TILE = 256
RET_CHUNKS_PER_ITER = 4
VMEM_LIMIT = 56 * 1024 * 1024

LOG2E = math.log2(math.e)
DQ_SCALE = DIFF_HEAD_DIM ** -0.5 * LOG2E

F32 = jnp.float32
BF16 = jnp.bfloat16


def _rms(x, g):
    return x * lax.rsqrt(jnp.mean(x * x, axis=-1, keepdims=True) + NORM_EPS) * g


def _dot(a, b):
    return jnp.dot(a, b, preferred_element_type=F32)


def _dot_nt(a, b):
    return lax.dot_general(a, b, (((1,), (1,)), ((), ())), preferred_element_type=F32)


def _dot_tn(a, b):
    return lax.dot_general(a, b, (((0,), (0,)), ((), ())), preferred_element_type=F32)


def _resident(shape):
    return pl.BlockSpec(shape, lambda *_: (0,) * len(shape), pipeline_mode=pl.Buffered(1))


def _inproj_kernel(x_ref, g_ref, w_ref, o_ref):
    sub = ROW_TILE // ROW_SUBTILES
    u = [_rms(x_ref[i * sub:(i + 1) * sub, :], g_ref[...]).astype(BF16) for i in range(ROW_SUBTILES)]
    step = DIFF_QK_W
    for i in range(ROW_SUBTILES):
        rows = slice(i * sub, (i + 1) * sub)
        for blk in range(IN_WIDTH // step):
            cols = slice(blk * step, (blk + 1) * step)
            y = _dot(u[i], w_ref[:, cols])
            if blk == _DQ_BLK512:
                y = y * DQ_SCALE
            o_ref[rows, cols] = y.astype(BF16)


def _inproj(x2, gain, w):
    m = x2.shape[0]
    return pl.pallas_call(
        _inproj_kernel,
        out_shape=jax.ShapeDtypeStruct((m, IN_WIDTH), BF16),
        grid=(m // ROW_TILE,),
        in_specs=[
            pl.BlockSpec((ROW_TILE, D_MODEL), lambda i: (i, 0)),
            _resident((1, D_MODEL)),
            _resident((D_MODEL, IN_WIDTH)),
        ],
        out_specs=pl.BlockSpec((ROW_TILE, IN_WIDTH), lambda i: (i, 0)),
        compiler_params=pltpu.CompilerParams(
            dimension_semantics=("arbitrary",), vmem_limit_bytes=VMEM_LIMIT),
        name="inproj",
    )(x2, gain.reshape(1, D_MODEL), w)


def _retention_tables():
    c = RET_CHUNK
    gam = 1.0 - np.exp2(-5.0 - np.arange(RET_HEADS, dtype=np.float64))
    log_g = np.log(gam)
    idx = np.arange(c, dtype=np.float64)
    rel = idx[:, None] - idx[None, :]
    decay = np.where(rel[None] >= 0, np.exp(np.maximum(rel, 0.0)[None] * log_g[:, None, None]), 0.0)
    qdec = np.exp((idx + 1.0)[:, None] * log_g[None, :])
    kdec = np.exp((c - 1 - idx)[:, None] * log_g[None, :])
    qdec = np.repeat(qdec, RET_QK_DIM, axis=1)
    kdec = np.repeat(kdec, RET_QK_DIM, axis=1)
    gchunk = np.exp(c * log_g)
    return (decay.astype(np.float32), qdec.astype(np.float32), kdec.astype(np.float32),
            tuple(float(g) for g in gchunk))


def _retention_kernel(q_ref, k_ref, v_ref, decay_ref, qdec_ref, kdec_ref, o_ref, state_ref, *, gchunk):
    c = RET_CHUNK
    n_chunks = q_ref.shape[0] // c
    n_pairs = RET_HEADS // 2
    state_ref[...] = jnp.zeros_like(state_ref)
    lane = lax.broadcasted_iota(jnp.int32, (c, LANES), 1)
    low_half = lane < RET_QK_DIM
    col = lax.broadcasted_iota(jnp.int32, (1, 2 * RET_V_DIM), 1)
    pair_decay = [jnp.where(col < RET_V_DIM, gchunk[2 * p], gchunk[2 * p + 1]) for p in range(n_pairs)]

    per_iter = RET_CHUNKS_PER_ITER if n_chunks % RET_CHUNKS_PER_ITER == 0 else 1

    def chunks(it, _):
        rows = [pl.ds(pl.multiple_of((it * per_iter + i) * c, c), c) for i in range(per_iter)]
        scores, updates, qd_all = [], [], []
        for r in rows:
            q_all = q_ref[r, :]
            k_all = k_ref[r, :] * jnp.asarray(RET_QK_DIM ** -0.5, BF16)
            qd_all.append((q_all.astype(F32) * qdec_ref[...]).astype(BF16))
            kd_all = (k_all.astype(F32) * kdec_ref[...]).astype(BF16)
            for p in range(n_pairs):
                cols = slice(p * LANES, (p + 1) * LANES)
                qp = q_all[:, cols]
                zero = jnp.zeros_like(qp)
                qm2 = jnp.concatenate([jnp.where(low_half, qp, zero), jnp.where(low_half, zero, qp)], axis=0)
                scores.append(_dot_nt(qm2, k_all[:, cols]))
            for p in range(n_pairs):
                cols = slice(p * LANES, (p + 1) * LANES)
                vcols = slice(p * 2 * RET_V_DIM, (p + 1) * 2 * RET_V_DIM)
                updates.append(_dot_tn(kd_all[:, cols], v_ref[r, vcols]))
        state = [state_ref[p] for p in range(n_pairs)]
        for i, r in enumerate(rows):
            before = [st.astype(BF16) for st in state]
            state = [pair_decay[p] * state[p] + updates[i * n_pairs + p] for p in range(n_pairs)]
            for h in range(RET_HEADS):
                p, hh = divmod(h, 2)
                cols = slice(p * LANES, (p + 1) * LANES)
                vcols = slice(h * RET_V_DIM, (h + 1) * RET_V_DIM)
                sd = (scores[i * n_pairs + p][hh * c:(hh + 1) * c] * decay_ref[h]).astype(BF16)
                qd = qd_all[i][:, cols]
                keep = low_half if hh == 0 else jnp.logical_not(low_half)
                qd = jnp.where(keep, qd, jnp.zeros_like(qd))
                lhs = jnp.concatenate([sd, qd], axis=1)
                rhs = jnp.concatenate([v_ref[r, vcols], before[p][:, hh * RET_V_DIM:(hh + 1) * RET_V_DIM]],
                                      axis=0)
                o_ref[r, vcols] = _dot(lhs, rhs).astype(BF16)
        for p in range(n_pairs):
            state_ref[p] = state[p]
        return 0

    lax.fori_loop(0, n_chunks // per_iter, chunks, 0)


def _retention(proj3):
    b, s, _ = proj3.shape
    decay, qdec, kdec, gchunk = _retention_tables()
    c = RET_CHUNK
    return pl.pallas_call(
        functools.partial(_retention_kernel, gchunk=gchunk),
        out_shape=jax.ShapeDtypeStruct((b, s, RET_V_W), BF16),
        grid=(b,),
        in_specs=[
            pl.BlockSpec((None, s, RET_Q_W), lambda i: (i, 0, _RQ_BLK256)),
            pl.BlockSpec((None, s, RET_Q_W), lambda i: (i, 0, _RK_BLK256)),
            pl.BlockSpec((None, s, RET_V_W), lambda i: (i, 0, _RV_BLK512)),
            _resident((RET_HEADS, c, c)),
            _resident((c, RET_Q_W)),
            _resident((c, RET_Q_W)),
        ],
        out_specs=pl.BlockSpec((None, s, RET_V_W), lambda i: (i, 0, 0)),
        scratch_shapes=[pltpu.VMEM((RET_HEADS // 2, LANES, 2 * RET_V_DIM), F32)],
        compiler_params=pltpu.CompilerParams(
            dimension_semantics=("arbitrary",), vmem_limit_bytes=VMEM_LIMIT),
        name="retention",
    )(proj3, proj3, proj3, jnp.asarray(decay), jnp.asarray(qdec), jnp.asarray(kdec))


ATTN_AUG_ROWS = 16
ATTN_RING = 4
ATTN_LOOKAHEAD = 2
_BIAS_TERMS = 3


def _bf16_pieces(x, n):
    out = []
    for _ in range(n):
        p = float(np.float32(x).astype(jnp.bfloat16))
        out.append(p)
        x -= p
    return out


def _steps_per_iter(n):
    return next(k for k in (7, 4, 2, 1) if n % k == 0)


def _diffattn_kernel(off_ref, lq1_ref, lk1_ref, lq2_ref, lk2_ref, q_ref, k_ref, v_ref, dn_ref,
                     o_ref, kaug_ref, qqt_ref, vt_ref, mask_ref, ring_ref, m_ref, acc_ref, *, lam_init, n_off):
    t = ATTN_TILE
    s_len = q_ref.shape[0]
    n_q = s_len // t
    d = DIFF_V_DIM
    lam = (jnp.exp(jnp.sum(lq1_ref[...] * lk1_ref[...], axis=-1, keepdims=True))
           - jnp.exp(jnp.sum(lq2_ref[...] * lk2_ref[...], axis=-1, keepdims=True)) + lam_init)
    dn = dn_ref[...] * (1.0 - lam_init)
    head_cols = [slice(h * LANES, (h + 1) * LANES) for h in range(DIFF_HEADS)]

    @pl.when(pl.program_id(0) == 0)
    def _():
        key = lax.broadcasted_iota(jnp.int32, (t, 2 * t), 0)
        qry = lax.broadcasted_iota(jnp.int32, (t, 2 * t), 1)
        causal = jnp.where(qry >= t, qry - t, qry) >= key
        mask_ref[...] = jnp.where(causal, 0.0, -jnp.inf)
        pos = lax.broadcasted_iota(jnp.int32, (s_len, LANES), 0)
        lane = lax.broadcasted_iota(jnp.int32, (s_len, LANES), 1)
        part = jnp.where(lane % 2 == 0, pos - pos % LANES, pos % LANES)
        part = jnp.where(lane < 2 * _BIAS_TERMS, part, 0).astype(F32).astype(BF16)
        row = lax.broadcasted_iota(jnp.int32, (LANES, 2 * t), 0)
        for h in range(DIFF_HEADS):
            kaug_ref[h, :, LANES:] = part
            slope = 2.0 ** (-8.0 * (h + 1) / DIFF_HEADS)
            coef = jnp.zeros((LANES, 2 * t), F32)
            for i, piece in enumerate(_bf16_pieces(LOG2E, _BIAS_TERMS)):
                coef = jnp.where(row // 2 == i, slope * piece, coef)
            for qb in range(n_q):
                qqt_ref[h, qb, LANES:, :] = coef.astype(BF16)
        vt_ref[:, :, d:, :] = jnp.ones((DIFF_HEADS, n_q, ATTN_AUG_ROWS, t), BF16)

    sub = lax.broadcasted_iota(jnp.int32, (LANES, t), 0)
    upper = sub < DIFF_HEAD_DIM

    def stage(jb, _):
        rows = pl.ds(pl.multiple_of(jb * t, t), t)
        for h in range(DIFF_HEADS):
            kaug_ref[h, rows, :LANES] = k_ref[rows, head_cols[h]]
            vt_ref[h, jb, :d, :] = v_ref[rows, head_cols[h]].T
            qt = q_ref[rows, head_cols[h]].T
            zero = jnp.zeros_like(qt)
            qqt_ref[h, jb, :LANES, :t] = jnp.where(upper, qt, zero)
            qqt_ref[h, jb, :LANES, t:] = jnp.where(upper, zero, qt)
            m_ref[h, jb] = jnp.full(m_ref.shape[2:], -jnp.inf, F32)
            acc_ref[h, jb] = jnp.zeros(acc_ref.shape[2:], F32)
        return 0

    lax.fori_loop(0, n_q, stage, 0)

    def run_steps(n_steps, pair_of, masked, final):
        per_iter = _steps_per_iter(n_steps)
        items = per_iter * DIFF_HEADS

        def scores(step, h, slot):
            qi, j = pair_of(jnp.minimum(step, n_steps - 1))
            s = _dot(kaug_ref[h, pl.ds(pl.multiple_of(j * t, t), t), :], qqt_ref[h, qi])
            if masked:
                s = s + mask_ref[...]
            ring_ref[slot] = s
            return jnp.max(s, axis=0, keepdims=True)

        def update(step, h, slot, s_max):
            qi, j = pair_of(step)
            s = ring_ref[slot]
            m_old = m_ref[h, qi]
            m_new = jnp.maximum(m_old, s_max)
            alpha = jnp.exp2(m_old - m_new)
            p = jnp.exp2(s - m_new).astype(BF16)
            m_ref[h, qi] = m_new
            acc = alpha * acc_ref[h, qi] + _dot(vt_ref[h, j], p)
            acc_ref[h, qi] = acc
            if final:
                finish(qi, h)

        def body(it, pending):
            pending = list(pending)
            base = it * per_iter
            for idx in range(items):
                ahead = idx + ATTN_LOOKAHEAD
                new = scores(base + ahead // DIFF_HEADS, ahead % DIFF_HEADS, ahead % ATTN_RING)
                update(base + idx // DIFF_HEADS, idx % DIFF_HEADS, idx % ATTN_RING, pending[0])
                pending = pending[1:] + [new]
            return tuple(pending)

        assert items % ATTN_RING == 0 and ATTN_LOOKAHEAD < ATTN_RING
        first = tuple(scores(i // DIFF_HEADS, i % DIFF_HEADS, i) for i in range(ATTN_LOOKAHEAD))
        lax.fori_loop(0, n_steps // per_iter, body, first)

    def finish(qb, h):
        inv = 1.0 / acc_ref[h, qb, d:d + 1, :]
        o = acc_ref[h, qb, :d, :t] * inv[:, :t] - acc_ref[h, qb, :d, t:] * (lam * inv[:, t:])
        o = o * lax.rsqrt(jnp.mean(o * o, axis=0, keepdims=True) + NORM_EPS) * dn
        o_ref[pl.ds(pl.multiple_of(qb * t, t), t), head_cols[h]] = o.T.astype(BF16)

    if n_off:
        run_steps(n_off, lambda step: (off_ref[0, step], off_ref[1, step]), masked=False, final=False)
    run_steps(n_q, lambda step: (step, step), masked=True, final=True)


def _diffattn(proj3, lq1, lk1, lq2, lk2, diff_norm, lam_init):
    b, s, _ = proj3.shape
    t = ATTN_TILE
    n_q = s // t
    off = np.array([(qi, j) for qi in range(n_q) for j in range(qi)], np.int32).reshape(-1, 2).T
    n_off = off.shape[1]
    if n_off == 0:
        off = np.zeros((2, 1), np.int32)
    vec = lambda a: a.reshape(1, DIFF_HEAD_DIM).astype(F32)
    rows_aug = DIFF_V_DIM + ATTN_AUG_ROWS
    return pl.pallas_call(
        functools.partial(_diffattn_kernel, lam_init=lam_init, n_off=n_off),
        out_shape=jax.ShapeDtypeStruct((b, s, DIFF_V_W), BF16),
        grid=(b,),
        in_specs=[
            pl.BlockSpec(memory_space=pltpu.SMEM),
            _resident((1, DIFF_HEAD_DIM)), _resident((1, DIFF_HEAD_DIM)),
            _resident((1, DIFF_HEAD_DIM)), _resident((1, DIFF_HEAD_DIM)),
            pl.BlockSpec((None, s, DIFF_QK_W), lambda i: (i, 0, _DQ_BLK512)),
            pl.BlockSpec((None, s, DIFF_QK_W), lambda i: (i, 0, _DK_BLK512)),
            pl.BlockSpec((None, s, DIFF_V_W), lambda i: (i, 0, _DV_BLK512)),
            _resident((DIFF_V_DIM, 1)),
        ],
        out_specs=pl.BlockSpec((None, s, DIFF_V_W), lambda i: (i, 0, 0)),
        scratch_shapes=[
            pltpu.VMEM((DIFF_HEADS, s, 2 * LANES), BF16),
            pltpu.VMEM((DIFF_HEADS, n_q, 2 * LANES, 2 * t), BF16),
            pltpu.VMEM((DIFF_HEADS, n_q, rows_aug, t), BF16),
            pltpu.VMEM((t, 2 * t), F32),
            pltpu.VMEM((ATTN_RING, t, 2 * t), F32),
            pltpu.VMEM((DIFF_HEADS, n_q, 1, 2 * t), F32),
            pltpu.VMEM((DIFF_HEADS, n_q, rows_aug, 2 * t), F32),
        ],
        compiler_params=pltpu.CompilerParams(
            dimension_semantics=("arbitrary",), vmem_limit_bytes=VMEM_LIMIT),
        name="diffattn",
    )(jnp.asarray(off), vec(lq1), vec(lk1), vec(lq2), vec(lk2), proj3, proj3, proj3,
      diff_norm.reshape(DIFF_V_DIM, 1))


def _silu(x):
    return x * (1.0 / (1.0 + jnp.exp(-x)))


def _mix_ffn_kernel(h_ref, ret_ref, gate_ref, dif_ref, rn_ref, wo_ref, fn_ref, wg_ref, wu_ref, wd_ref,
                    *rest, final):
    if final:
        final_ref, o_ref, a_ref = rest
    else:
        o_ref, a_ref = rest
    sub = ROW_TILE // ROW_SUBTILES
    tiles = [slice(i * sub, (i + 1) * sub) for i in range(ROW_SUBTILES)]
    h1 = []
    for r in tiles:
        x = h_ref[r, :] + _dot(dif_ref[r, :], wo_ref[RET_V_W:, :])
        heads = []
        for hd in range(RET_HEADS):
            cols = slice(hd * RET_V_DIM, (hd + 1) * RET_V_DIM)
            ret = _rms(ret_ref[r, cols].astype(F32), rn_ref[...])
            heads.append((_silu(gate_ref[r, cols].astype(F32)) * ret).astype(BF16))
        h1.append(x + _dot(jnp.concatenate(heads, axis=-1), wo_ref[:RET_V_W, :]))
    u = [_rms(x, fn_ref[...]).astype(BF16) for x in h1]
    step = 1024
    for i, r in enumerate(tiles):
        for c0 in range(0, FFN_HIDDEN, step):
            c1 = min(c0 + step, FFN_HIDDEN)
            up = _dot(u[i], wu_ref[:, c0:c1])
            a_ref[r, c0:c1] = (_silu(_dot(u[i], wg_ref[:, c0:c1])) * up).astype(BF16)
        out = h1[i] + _dot(a_ref[r, :], wd_ref[...])
        if final:
            out = _rms(out, final_ref[...])
        o_ref[r, :] = out


def _mix_ffn(h2, ret2, proj, dif2, rn, wo, fn, wg, wu, wd, final_norm):
    m = h2.shape[0]
    final = final_norm is not None
    row = lambda w, blk=0: pl.BlockSpec((ROW_TILE, w), lambda i: (i, blk))
    in_specs = [
        row(D_MODEL), row(RET_V_W), row(RET_V_W, _RG_BLK512), row(DIFF_V_W),
        _resident((1, RET_V_DIM)),
        _resident((MIX_WIDTH, D_MODEL)),
        _resident((1, D_MODEL)),
        _resident((D_MODEL, FFN_HIDDEN)),
        _resident((D_MODEL, FFN_HIDDEN)),
        _resident((FFN_HIDDEN, D_MODEL)),
    ]
    args = [h2, ret2, proj, dif2, rn.reshape(1, RET_V_DIM), wo, fn.reshape(1, D_MODEL), wg, wu, wd]
    if final:
        in_specs.append(_resident((1, D_MODEL)))
        args.append(final_norm.reshape(1, D_MODEL))
    return pl.pallas_call(
        functools.partial(_mix_ffn_kernel, final=final),
        out_shape=jax.ShapeDtypeStruct((m, D_MODEL), F32),
        grid=(m // ROW_TILE,),
        in_specs=in_specs,
        out_specs=row(D_MODEL),
        scratch_shapes=[pltpu.VMEM((ROW_TILE, FFN_HIDDEN), BF16)],
        compiler_params=pltpu.CompilerParams(
            dimension_semantics=("arbitrary",), vmem_limit_bytes=VMEM_LIMIT),
        name="mix_ffn_final" if final else "mix_ffn",
    )(*args)


def kernel(x, attn_norm, w_in, ret_norm, lambda_q1, lambda_k1, lambda_q2, lambda_k2,
           diff_norm, w_out, ffn_norm, w_gate, w_up, w_down, final_norm):
    b, s, _ = x.shape
    depth = w_in.shape[0]
    assert (b * s) % ROW_TILE == 0 and s % ATTN_TILE == 0 and s % RET_CHUNK == 0
    h = x.reshape(b * s, D_MODEL)
    for l in range(depth):
        proj = _inproj(h, attn_norm[l], w_in[l].astype(BF16))
        proj3 = proj.reshape(b, s, IN_WIDTH)
        ret = _retention(proj3)
        lam_init = 0.8 - 0.6 * math.exp(-0.3 * l)
        dif = _diffattn(proj3, lambda_q1[l], lambda_k1[l], lambda_q2[l], lambda_k2[l],
                        diff_norm[l], lam_init)
        h = _mix_ffn(h, ret.reshape(b * s, RET_V_W), proj, dif.reshape(b * s, DIFF_V_W), ret_norm[l],
                     w_out[l].astype(BF16), ffn_norm[l], w_gate[l].astype(BF16),
                     w_up[l].astype(BF16), w_down[l].astype(BF16),
                     final_norm if l == depth - 1 else None)
    return h.reshape(b, s, D_MODEL)
```

```python
import functools
import math

import numpy as np
import jax
import jax.numpy as jnp
from jax import lax
from jax.experimental import pallas as pl
from jax.experimental.pallas import tpu as pltpu

D_MODEL = 1024
RET_HEADS = 4
RET_QK_DIM = 64
RET_V_DIM = 128
RET_CHUNK = 128
DIFF_HEADS = 4
DIFF_HEAD_DIM = 64
DIFF_V_DIM = 2 * DIFF_HEAD_DIM
FFN_HIDDEN = 2816
NORM_EPS = 1e-6

RET_Q_W = RET_HEADS * RET_QK_DIM
RET_V_W = RET_HEADS * RET_V_DIM
DIFF_QK_W = DIFF_HEADS * 2 * DIFF_HEAD_DIM
DIFF_V_W = DIFF_HEADS * DIFF_V_DIM
IN_WIDTH = 2 * RET_Q_W + 2 * RET_V_W + 2 * DIFF_QK_W + DIFF_V_W
MIX_WIDTH = RET_V_W + DIFF_V_W

LANES = 128
_RQ_BLK256 = 0
_RK_BLK256 = 1
_RV_BLK512 = 1
_RG_BLK512 = 2
_DQ_BLK512 = 3
_DK_BLK512 = 4
_DV_BLK512 = 5

ROW_TILE = 1024
ROW_SUBTILES = 4
ATTN_TILE = 256
RET_CHUNKS_PER_ITER = 4
FFN_CHUNK = 1024
VMEM_LIMIT = 56 * 1024 * 1024

LOG2E = math.log2(math.e)
DQ_SCALE = DIFF_HEAD_DIM ** -0.5 * LOG2E

F32 = jnp.float32
BF16 = jnp.bfloat16


def _rms(x, g):
    return x * lax.rsqrt(jnp.mean(x * x, axis=-1, keepdims=True) + NORM_EPS) * g


def _dot(a, b):
    return jnp.dot(a, b, preferred_element_type=F32)


def _dot_nt(a, b):
    return lax.dot_general(a, b, (((1,), (1,)), ((), ())), preferred_element_type=F32)


def _dot_tn(a, b):
    return lax.dot_general(a, b, (((0,), (0,)), ((), ())), preferred_element_type=F32)


def _resident(shape):
    return pl.BlockSpec(shape, lambda *_: (0,) * len(shape), pipeline_mode=pl.Buffered(1))


def _inproj_kernel(x_ref, g_ref, w_ref, o_ref):
    sub = ROW_TILE // ROW_SUBTILES
    u = [_rms(x_ref[i * sub:(i + 1) * sub, :], g_ref[...]).astype(BF16) for i in range(ROW_SUBTILES)]
    step = DIFF_QK_W
    for i in range(ROW_SUBTILES):
        rows = slice(i * sub, (i + 1) * sub)
        for blk in range(IN_WIDTH // step):
            cols = slice(blk * step, (blk + 1) * step)
            y = _dot(u[i], w_ref[:, cols])
            if blk == _DQ_BLK512:
                y = y * DQ_SCALE
            o_ref[rows, cols] = y.astype(BF16)


def _inproj(x2, gain, w):
    m = x2.shape[0]
    return pl.pallas_call(
        _inproj_kernel,
        out_shape=jax.ShapeDtypeStruct((m, IN_WIDTH), BF16),
        grid=(m // ROW_TILE,),
        in_specs=[
            pl.BlockSpec((ROW_TILE, D_MODEL), lambda i: (i, 0)),
            _resident((1, D_MODEL)),
            _resident((D_MODEL, IN_WIDTH)),
        ],
        out_specs=pl.BlockSpec((ROW_TILE, IN_WIDTH), lambda i: (i, 0)),
        compiler_params=pltpu.CompilerParams(
            dimension_semantics=("arbitrary",), vmem_limit_bytes=VMEM_LIMIT),
        name="inproj",
    )(x2, gain.reshape(1, D_MODEL), w)


def _retention_tables():
    c = RET_CHUNK
    gam = 1.0 - np.exp2(-5.0 - np.arange(RET_HEADS, dtype=np.float64))
    log_g = np.log(gam)
    idx = np.arange(c, dtype=np.float64)
    rel = idx[:, None] - idx[None, :]
    decay = np.where(rel[None] >= 0, np.exp(np.maximum(rel, 0.0)[None] * log_g[:, None, None]), 0.0)
    qdec = np.exp((idx + 1.0)[:, None] * log_g[None, :])
    kdec = np.exp((c - 1 - idx)[:, None] * log_g[None, :])
    qdec = np.repeat(qdec, RET_QK_DIM, axis=1)
    kdec = np.repeat(kdec, RET_QK_DIM, axis=1)
    gchunk = np.exp(c * log_g)
    return (decay.astype(np.float32), qdec.astype(np.float32), kdec.astype(np.float32),
            tuple(float(g) for g in gchunk))


def _retention_kernel(q_ref, k_ref, v_ref, decay_ref, qdec_ref, kdec_ref, o_ref, state_ref, *, gchunk):
    c = RET_CHUNK
    n_chunks = q_ref.shape[0] // c
    n_pairs = RET_HEADS // 2
    state_ref[...] = jnp.zeros_like(state_ref)
    lane = lax.broadcasted_iota(jnp.int32, (c, LANES), 1)
    low_half = lane < RET_QK_DIM
    col = lax.broadcasted_iota(jnp.int32, (1, 2 * RET_V_DIM), 1)
    pair_decay = [jnp.where(col < RET_V_DIM, gchunk[2 * p], gchunk[2 * p + 1]) for p in range(n_pairs)]

    per_iter = RET_CHUNKS_PER_ITER if n_chunks % RET_CHUNKS_PER_ITER == 0 else 1

    def chunks(it, _):
        rows = [pl.ds(pl.multiple_of((it * per_iter + i) * c, c), c) for i in range(per_iter)]
        scores, updates, qd_all = [], [], []
        for r in rows:
            q_all = q_ref[r, :]
            k_all = k_ref[r, :] * jnp.asarray(RET_QK_DIM ** -0.5, BF16)
            qd_all.append((q_all.astype(F32) * qdec_ref[...]).astype(BF16))
            kd_all = (k_all.astype(F32) * kdec_ref[...]).astype(BF16)
            for p in range(n_pairs):
                cols = slice(p * LANES, (p + 1) * LANES)
                qp = q_all[:, cols]
                zero = jnp.zeros_like(qp)
                qm2 = jnp.concatenate([jnp.where(low_half, qp, zero), jnp.where(low_half, zero, qp)], axis=0)
                scores.append(_dot_nt(qm2, k_all[:, cols]))
            for p in range(n_pairs):
                cols = slice(p * LANES, (p + 1) * LANES)
                vcols = slice(p * 2 * RET_V_DIM, (p + 1) * 2 * RET_V_DIM)
                updates.append(_dot_tn(kd_all[:, cols], v_ref[r, vcols]))
        state = [state_ref[p] for p in range(n_pairs)]
        for i, r in enumerate(rows):
            before = [st.astype(BF16) for st in state]
            state = [pair_decay[p] * state[p] + updates[i * n_pairs + p] for p in range(n_pairs)]
            for h in range(RET_HEADS):
                p, hh = divmod(h, 2)
                cols = slice(p * LANES, (p + 1) * LANES)
                vcols = slice(h * RET_V_DIM, (h + 1) * RET_V_DIM)
                sd = (scores[i * n_pairs + p][hh * c:(hh + 1) * c] * decay_ref[h]).astype(BF16)
                qd = qd_all[i][:, cols]
                keep = low_half if hh == 0 else jnp.logical_not(low_half)
                qd = jnp.where(keep, qd, jnp.zeros_like(qd))
                lhs = jnp.concatenate([sd, qd], axis=1)
                rhs = jnp.concatenate([v_ref[r, vcols], before[p][:, hh * RET_V_DIM:(hh + 1) * RET_V_DIM]],
                                      axis=0)
                o_ref[r, vcols] = _dot(lhs, rhs).astype(BF16)
        for p in range(n_pairs):
            state_ref[p] = state[p]
        return 0

    lax.fori_loop(0, n_chunks // per_iter, chunks, 0)


def _retention(proj3):
    b, s, _ = proj3.shape
    decay, qdec, kdec, gchunk = _retention_tables()
    c = RET_CHUNK
    return pl.pallas_call(
        functools.partial(_retention_kernel, gchunk=gchunk),
        out_shape=jax.ShapeDtypeStruct((b, s, RET_V_W), BF16),
        grid=(b,),
        in_specs=[
            pl.BlockSpec((None, s, RET_Q_W), lambda i: (i, 0, _RQ_BLK256)),
            pl.BlockSpec((None, s, RET_Q_W), lambda i: (i, 0, _RK_BLK256)),
            pl.BlockSpec((None, s, RET_V_W), lambda i: (i, 0, _RV_BLK512)),
            _resident((RET_HEADS, c, c)),
            _resident((c, RET_Q_W)),
            _resident((c, RET_Q_W)),
        ],
        out_specs=pl.BlockSpec((None, s, RET_V_W), lambda i: (i, 0, 0)),
        scratch_shapes=[pltpu.VMEM((RET_HEADS // 2, LANES, 2 * RET_V_DIM), F32)],
        compiler_params=pltpu.CompilerParams(
            dimension_semantics=("arbitrary",), vmem_limit_bytes=VMEM_LIMIT),
        name="retention",
    )(proj3, proj3, proj3, jnp.asarray(decay), jnp.asarray(qdec), jnp.asarray(kdec))


ATTN_AUG_ROWS = 16
ATTN_RING = 4
ATTN_LOOKAHEAD = 2
_BIAS_TERMS = 3


def _bf16_pieces(x, n):
    out = []
    for _ in range(n):
        p = float(np.float32(x).astype(jnp.bfloat16))
        out.append(p)
        x -= p
    return out


def _steps_per_iter(n):
    return next(k for k in (7, 4, 2, 1) if n % k == 0)


def _diffattn_kernel(off_ref, lq1_ref, lk1_ref, lq2_ref, lk2_ref, q_ref, k_ref, v_ref,
                     o_ref, kaug_ref, qqt_ref, vt_ref, mask_ref, ring_ref, m_ref, acc_ref, *, lam_init, n_off):
    t = ATTN_TILE
    s_len = q_ref.shape[0]
    n_q = s_len // t
    d = DIFF_V_DIM
    lam = (jnp.exp(jnp.sum(lq1_ref[...] * lk1_ref[...], axis=-1, keepdims=True))
           - jnp.exp(jnp.sum(lq2_ref[...] * lk2_ref[...], axis=-1, keepdims=True)) + lam_init)
    head_cols = [slice(h * LANES, (h + 1) * LANES) for h in range(DIFF_HEADS)]

    @pl.when(pl.program_id(0) == 0)
    def _():
        key = lax.broadcasted_iota(jnp.int32, (t, 2 * t), 0)
        qry = lax.broadcasted_iota(jnp.int32, (t, 2 * t), 1)
        causal = jnp.where(qry >= t, qry - t, qry) >= key
        mask_ref[...] = jnp.where(causal, 0.0, -jnp.inf)
        pos = lax.broadcasted_iota(jnp.int32, (s_len, LANES), 0)
        lane = lax.broadcasted_iota(jnp.int32, (s_len, LANES), 1)
        part = jnp.where(lane % 2 == 0, pos - pos % LANES, pos % LANES)
        part = jnp.where(lane < 2 * _BIAS_TERMS, part, 0).astype(F32).astype(BF16)
        row = lax.broadcasted_iota(jnp.int32, (LANES, 2 * t), 0)
        for h in range(DIFF_HEADS):
            kaug_ref[h, :, LANES:] = part
            slope = 2.0 ** (-8.0 * (h + 1) / DIFF_HEADS)
            coef = jnp.zeros((LANES, 2 * t), F32)
            for i, piece in enumerate(_bf16_pieces(LOG2E, _BIAS_TERMS)):
                coef = jnp.where(row // 2 == i, slope * piece, coef)
            for qb in range(n_q):
                qqt_ref[h, qb, LANES:, :] = coef.astype(BF16)
        vt_ref[:, :, d:, :] = jnp.ones((DIFF_HEADS, n_q, ATTN_AUG_ROWS, t), BF16)

    sub = lax.broadcasted_iota(jnp.int32, (LANES, t), 0)
    upper = sub < DIFF_HEAD_DIM

    def stage(jb, _):
        rows = pl.ds(pl.multiple_of(jb * t, t), t)
        for h in range(DIFF_HEADS):
            kaug_ref[h, rows, :LANES] = k_ref[rows, head_cols[h]]
            vt_ref[h, jb, :d, :] = v_ref[rows, head_cols[h]].T
            qt = q_ref[rows, head_cols[h]].T
            zero = jnp.zeros_like(qt)
            qqt_ref[h, jb, :LANES, :t] = jnp.where(upper, qt, zero)
            qqt_ref[h, jb, :LANES, t:] = jnp.where(upper, zero, qt)
            m_ref[h, jb] = jnp.full(m_ref.shape[2:], -jnp.inf, F32)
            acc_ref[h, jb] = jnp.zeros(acc_ref.shape[2:], F32)
        return 0

    lax.fori_loop(0, n_q, stage, 0)

    def run_steps(n_steps, pair_of, masked, final):
        per_iter = _steps_per_iter(n_steps)
        items = per_iter * DIFF_HEADS

        def scores(step, h, slot):
            qi, j = pair_of(jnp.minimum(step, n_steps - 1))
            s = _dot(kaug_ref[h, pl.ds(pl.multiple_of(j * t, t), t), :], qqt_ref[h, qi])
            if masked:
                s = s + mask_ref[...]
            ring_ref[slot] = s
            return jnp.max(s, axis=0, keepdims=True)

        def update(step, h, slot, s_max):
            qi, j = pair_of(step)
            s = ring_ref[slot]
            m_old = m_ref[h, qi]
            m_new = jnp.maximum(m_old, s_max)
            alpha = jnp.exp2(m_old - m_new)
            p = jnp.exp2(s - m_new).astype(BF16)
            m_ref[h, qi] = m_new
            acc = alpha * acc_ref[h, qi] + _dot(vt_ref[h, j], p)
            acc_ref[h, qi] = acc
            if final:
                finish(qi, h)

        def body(it, pending):
            pending = list(pending)
            base = it * per_iter
            for idx in range(items):
                ahead = idx + ATTN_LOOKAHEAD
                new = scores(base + ahead // DIFF_HEADS, ahead % DIFF_HEADS, ahead % ATTN_RING)
                update(base + idx // DIFF_HEADS, idx % DIFF_HEADS, idx % ATTN_RING, pending[0])
                pending = pending[1:] + [new]
            return tuple(pending)

        assert items % ATTN_RING == 0 and ATTN_LOOKAHEAD < ATTN_RING
        first = tuple(scores(i // DIFF_HEADS, i % DIFF_HEADS, i) for i in range(ATTN_LOOKAHEAD))
        lax.fori_loop(0, n_steps // per_iter, body, first)

    def finish(qb, h):
        inv = 1.0 / acc_ref[h, qb, d:d + 1, :]
        o = acc_ref[h, qb, :d, :t] * inv[:, :t] - acc_ref[h, qb, :d, t:] * (lam * inv[:, t:])
        o_ref[pl.ds(pl.multiple_of(qb * t, t), t), head_cols[h]] = o.T.astype(BF16)

    if n_off:
        run_steps(n_off, lambda step: (off_ref[0, step], off_ref[1, step]), masked=False, final=False)
    run_steps(n_q, lambda step: (step, step), masked=True, final=True)


def _diffattn(proj3, lq1, lk1, lq2, lk2, lam_init):
    b, s, _ = proj3.shape
    t = ATTN_TILE
    n_q = s // t
    off = np.array([(qi, j) for qi in range(n_q) for j in range(qi)], np.int32).reshape(-1, 2).T
    n_off = off.shape[1]
    if n_off == 0:
        off = np.zeros((2, 1), np.int32)
    vec = lambda a: a.reshape(1, DIFF_HEAD_DIM).astype(F32)
    rows_aug = DIFF_V_DIM + ATTN_AUG_ROWS
    return pl.pallas_call(
        functools.partial(_diffattn_kernel, lam_init=lam_init, n_off=n_off),
        out_shape=jax.ShapeDtypeStruct((b, s, DIFF_V_W), BF16),
        grid=(b,),
        in_specs=[
            pl.BlockSpec(memory_space=pltpu.SMEM),
            _resident((1, DIFF_HEAD_DIM)), _resident((1, DIFF_HEAD_DIM)),
            _resident((1, DIFF_HEAD_DIM)), _resident((1, DIFF_HEAD_DIM)),
            pl.BlockSpec((None, s, DIFF_QK_W), lambda i: (i, 0, _DQ_BLK512)),
            pl.BlockSpec((None, s, DIFF_QK_W), lambda i: (i, 0, _DK_BLK512)),
            pl.BlockSpec((None, s, DIFF_V_W), lambda i: (i, 0, _DV_BLK512)),
        ],
        out_specs=pl.BlockSpec((None, s, DIFF_V_W), lambda i: (i, 0, 0)),
        scratch_shapes=[
            pltpu.VMEM((DIFF_HEADS, s, 2 * LANES), BF16),
            pltpu.VMEM((DIFF_HEADS, n_q, 2 * LANES, 2 * t), BF16),
            pltpu.VMEM((DIFF_HEADS, n_q, rows_aug, t), BF16),
            pltpu.VMEM((t, 2 * t), F32),
            pltpu.VMEM((ATTN_RING, t, 2 * t), F32),
            pltpu.VMEM((DIFF_HEADS, n_q, 1, 2 * t), F32),
            pltpu.VMEM((DIFF_HEADS, n_q, rows_aug, 2 * t), F32),
        ],
        compiler_params=pltpu.CompilerParams(
            dimension_semantics=("arbitrary",), vmem_limit_bytes=VMEM_LIMIT),
        name="diffattn",
    )(jnp.asarray(off), vec(lq1), vec(lk1), vec(lq2), vec(lk2), proj3, proj3, proj3)


def _silu(x):
    return x * (1.0 / (1.0 + jnp.exp(-x)))


def _mix_ffn_kernel(h_ref, ret_ref, gate_ref, dif_ref, rn_ref, dn_ref, wo_ref, fn_ref, wg_ref, wu_ref,
                    wd_ref, *rest, final, dif_scale):
    if final:
        final_ref, o_ref, a_ref = rest
    else:
        o_ref, a_ref = rest
    sub = ROW_TILE // ROW_SUBTILES
    tiles = [slice(i * sub, (i + 1) * sub) for i in range(ROW_SUBTILES)]
    dn = dn_ref[...] * dif_scale
    h1 = []
    for r in tiles:
        dif = [_rms(dif_ref[r, hd * DIFF_V_DIM:(hd + 1) * DIFF_V_DIM].astype(F32), dn).astype(BF16)
               for hd in range(DIFF_HEADS)]
        x = h_ref[r, :] + _dot(jnp.concatenate(dif, axis=-1), wo_ref[RET_V_W:, :])
        heads = []
        for hd in range(RET_HEADS):
            cols = slice(hd * RET_V_DIM, (hd + 1) * RET_V_DIM)
            ret = _rms(ret_ref[r, cols].astype(F32), rn_ref[...])
            heads.append((_silu(gate_ref[r, cols].astype(F32)) * ret).astype(BF16))
        h1.append(x + _dot(jnp.concatenate(heads, axis=-1), wo_ref[:RET_V_W, :]))
    u = [_rms(x, fn_ref[...]).astype(BF16) for x in h1]
    for i, r in enumerate(tiles):
        for c0 in range(0, FFN_HIDDEN, FFN_CHUNK):
            c1 = min(c0 + FFN_CHUNK, FFN_HIDDEN)
            up = _dot(u[i], wu_ref[:, c0:c1])
            a_ref[r, c0:c1] = (_silu(_dot(u[i], wg_ref[:, c0:c1])) * up).astype(BF16)
        out = h1[i] + _dot(a_ref[r, :], wd_ref[...])
        if final:
            out = _rms(out, final_ref[...])
        o_ref[r, :] = out


def _mix_ffn(h2, ret2, proj, dif2, rn, dn, dif_scale, wo, fn, wg, wu, wd, final_norm):
    m = h2.shape[0]
    final = final_norm is not None
    row = lambda w, blk=0: pl.BlockSpec((ROW_TILE, w), lambda i: (i, blk))
    in_specs = [
        row(D_MODEL), row(RET_V_W), row(RET_V_W, _RG_BLK512), row(DIFF_V_W),
        _resident((1, RET_V_DIM)),
        _resident((1, DIFF_V_DIM)),
        _resident((MIX_WIDTH, D_MODEL)),
        _resident((1, D_MODEL)),
        _resident((D_MODEL, FFN_HIDDEN)),
        _resident((D_MODEL, FFN_HIDDEN)),
        _resident((FFN_HIDDEN, D_MODEL)),
    ]
    args = [h2, ret2, proj, dif2, rn.reshape(1, RET_V_DIM), dn.reshape(1, DIFF_V_DIM), wo,
            fn.reshape(1, D_MODEL), wg, wu, wd]
    if final:
        in_specs.append(_resident((1, D_MODEL)))
        args.append(final_norm.reshape(1, D_MODEL))
    return pl.pallas_call(
        functools.partial(_mix_ffn_kernel, final=final, dif_scale=dif_scale),
        out_shape=jax.ShapeDtypeStruct((m, D_MODEL), F32),
        grid=(m // ROW_TILE,),
        in_specs=in_specs,
        out_specs=row(D_MODEL),
        scratch_shapes=[pltpu.VMEM((ROW_TILE, FFN_HIDDEN), BF16)],
        compiler_params=pltpu.CompilerParams(
            dimension_semantics=("arbitrary",), vmem_limit_bytes=VMEM_LIMIT),
        name="mix_ffn_final" if final else "mix_ffn",
    )(*args)


def kernel(x, attn_norm, w_in, ret_norm, lambda_q1, lambda_k1, lambda_q2, lambda_k2,
           diff_norm, w_out, ffn_norm, w_gate, w_up, w_down, final_norm):
    b, s, _ = x.shape
    depth = w_in.shape[0]
    assert (b * s) % ROW_TILE == 0 and s % ATTN_TILE == 0 and s % RET_CHUNK == 0
    h = x.reshape(b * s, D_MODEL)
    for l in range(depth):
        proj = _inproj(h, attn_norm[l], w_in[l].astype(BF16))
        proj3 = proj.reshape(b, s, IN_WIDTH)
        ret = _retention(proj3)
        lam_init = 0.8 - 0.6 * math.exp(-0.3 * l)
        dif = _diffattn(proj3, lambda_q1[l], lambda_k1[l], lambda_q2[l], lambda_k2[l], lam_init)
        h = _mix_ffn(h, ret.reshape(b * s, RET_V_W), proj, dif.reshape(b * s, DIFF_V_W), ret_norm[l],
                     diff_norm[l], 1.0 - lam_init, w_out[l].astype(BF16), ffn_norm[l], w_gate[l].astype(BF16),
                     w_up[l].astype(BF16), w_down[l].astype(BF16),
                     final_norm if l == depth - 1 else None)
    return h.reshape(b, s, D_MODEL)
```

```python
import functools
import math

import numpy as np
import jax
import jax.numpy as jnp
from jax import lax
from jax.experimental import pallas as pl
from jax.experimental.pallas import tpu as pltpu

D_MODEL = 1024
RET_HEADS = 4
RET_QK_DIM = 64
RET_V_DIM = 128
RET_CHUNK = 128
DIFF_HEADS = 4
DIFF_HEAD_DIM = 64
DIFF_V_DIM = 2 * DIFF_HEAD_DIM
FFN_HIDDEN = 2816
NORM_EPS = 1e-6

RET_Q_W = RET_HEADS * RET_QK_DIM
RET_V_W = RET_HEADS * RET_V_DIM
DIFF_QK_W = DIFF_HEADS * 2 * DIFF_HEAD_DIM
DIFF_V_W = DIFF_HEADS * DIFF_V_DIM
IN_WIDTH = 2 * RET_Q_W + 2 * RET_V_W + 2 * DIFF_QK_W + DIFF_V_W
MIX_WIDTH = RET_V_W + DIFF_V_W

LANES = 128
_RQ_BLK256 = 0
_RK_BLK256 = 1
_RV_BLK512 = 1
_RG_BLK512 = 2
_DQ_BLK512 = 3
_DK_BLK512 = 4
_DV_BLK512 = 5

ROW_TILE = 1024
ROW_SUBTILES = 4
ATTN_TILE = 256
RET_CHUNKS_PER_ITER = 8
FFN_CHUNK = 512
VMEM_LIMIT = 56 * 1024 * 1024

LOG2E = math.log2(math.e)
DQ_SCALE = DIFF_HEAD_DIM ** -0.5 * LOG2E

F32 = jnp.float32
BF16 = jnp.bfloat16


def _rms(x, g):
    return x * lax.rsqrt(jnp.mean(x * x, axis=-1, keepdims=True) + NORM_EPS) * g


def _dot(a, b):
    return jnp.dot(a, b, preferred_element_type=F32)


def _dot_nt(a, b):
    return lax.dot_general(a, b, (((1,), (1,)), ((), ())), preferred_element_type=F32)


def _dot_tn(a, b):
    return lax.dot_general(a, b, (((0,), (0,)), ((), ())), preferred_element_type=F32)


def _resident(shape):
    return pl.BlockSpec(shape, lambda *_: (0,) * len(shape), pipeline_mode=pl.Buffered(1))


def _inproj_kernel(x_ref, g_ref, w_ref, o_ref):
    sub = ROW_TILE // ROW_SUBTILES
    u = [_rms(x_ref[i * sub:(i + 1) * sub, :], g_ref[...]).astype(BF16) for i in range(ROW_SUBTILES)]
    step = DIFF_QK_W
    for i in range(ROW_SUBTILES):
        rows = slice(i * sub, (i + 1) * sub)
        for blk in range(IN_WIDTH // step):
            cols = slice(blk * step, (blk + 1) * step)
            y = _dot(u[i], w_ref[:, cols])
            if blk == _DQ_BLK512:
                y = y * DQ_SCALE
            o_ref[rows, cols] = y.astype(BF16)


def _inproj(x2, gain, w):
    m = x2.shape[0]
    return pl.pallas_call(
        _inproj_kernel,
        out_shape=jax.ShapeDtypeStruct((m, IN_WIDTH), BF16),
        grid=(m // ROW_TILE,),
        in_specs=[
            pl.BlockSpec((ROW_TILE, D_MODEL), lambda i: (i, 0)),
            _resident((1, D_MODEL)),
            _resident((D_MODEL, IN_WIDTH)),
        ],
        out_specs=pl.BlockSpec((ROW_TILE, IN_WIDTH), lambda i: (i, 0)),
        compiler_params=pltpu.CompilerParams(
            dimension_semantics=("arbitrary",), vmem_limit_bytes=VMEM_LIMIT),
        name="inproj",
    )(x2, gain.reshape(1, D_MODEL), w)


def _retention_tables():
    c = RET_CHUNK
    gam = 1.0 - np.exp2(-5.0 - np.arange(RET_HEADS, dtype=np.float64))
    log_g = np.log(gam)
    idx = np.arange(c, dtype=np.float64)
    rel = idx[:, None] - idx[None, :]
    decay = np.where(rel[None] >= 0, np.exp(np.maximum(rel, 0.0)[None] * log_g[:, None, None]), 0.0)
    qdec = np.exp((idx + 1.0)[:, None] * log_g[None, :])
    kdec = np.exp((c - 1 - idx)[:, None] * log_g[None, :])
    qdec = np.repeat(qdec, RET_QK_DIM, axis=1)
    kdec = np.repeat(kdec, RET_QK_DIM, axis=1)
    gchunk = np.exp(c * log_g)
    return (decay.astype(np.float32), qdec.astype(np.float32), kdec.astype(np.float32),
            tuple(float(g) for g in gchunk))


def _retention_kernel(q_ref, k_ref, v_ref, decay_ref, qdec_ref, kdec_ref, o_ref, state_ref, *, gchunk):
    c = RET_CHUNK
    n_chunks = q_ref.shape[0] // c
    n_pairs = RET_HEADS // 2
    state_ref[...] = jnp.zeros_like(state_ref)
    lane = lax.broadcasted_iota(jnp.int32, (c, LANES), 1)
    low_half = lane < RET_QK_DIM
    col = lax.broadcasted_iota(jnp.int32, (1, 2 * RET_V_DIM), 1)
    pair_decay = [jnp.where(col < RET_V_DIM, gchunk[2 * p], gchunk[2 * p + 1]) for p in range(n_pairs)]

    per_iter = RET_CHUNKS_PER_ITER if n_chunks % RET_CHUNKS_PER_ITER == 0 else 1

    def chunks(it, _):
        rows = [pl.ds(pl.multiple_of((it * per_iter + i) * c, c), c) for i in range(per_iter)]
        scores, updates, qd_all = [], [], []
        for r in rows:
            q_all = q_ref[r, :]
            k_all = k_ref[r, :] * jnp.asarray(RET_QK_DIM ** -0.5, BF16)
            qd_all.append((q_all.astype(F32) * qdec_ref[...]).astype(BF16))
            kd_all = (k_all.astype(F32) * kdec_ref[...]).astype(BF16)
            for p in range(n_pairs):
                cols = slice(p * LANES, (p + 1) * LANES)
                qp = q_all[:, cols]
                zero = jnp.zeros_like(qp)
                qm2 = jnp.concatenate([jnp.where(low_half, qp, zero), jnp.where(low_half, zero, qp)], axis=0)
                scores.append(_dot_nt(qm2, k_all[:, cols]))
            for p in range(n_pairs):
                cols = slice(p * LANES, (p + 1) * LANES)
                vcols = slice(p * 2 * RET_V_DIM, (p + 1) * 2 * RET_V_DIM)
                updates.append(_dot_tn(kd_all[:, cols], v_ref[r, vcols]))
        state = [state_ref[p] for p in range(n_pairs)]
        for i, r in enumerate(rows):
            before = [st.astype(BF16) for st in state]
            state = [pair_decay[p] * state[p] + updates[i * n_pairs + p] for p in range(n_pairs)]
            for h in range(RET_HEADS):
                p, hh = divmod(h, 2)
                cols = slice(p * LANES, (p + 1) * LANES)
                vcols = slice(h * RET_V_DIM, (h + 1) * RET_V_DIM)
                sd = (scores[i * n_pairs + p][hh * c:(hh + 1) * c] * decay_ref[h]).astype(BF16)
                qd = qd_all[i][:, cols]
                keep = low_half if hh == 0 else jnp.logical_not(low_half)
                qd = jnp.where(keep, qd, jnp.zeros_like(qd))
                lhs = jnp.concatenate([sd, qd], axis=1)
                rhs = jnp.concatenate([v_ref[r, vcols], before[p][:, hh * RET_V_DIM:(hh + 1) * RET_V_DIM]],
                                      axis=0)
                o_ref[r, vcols] = _dot(lhs, rhs).astype(BF16)
        for p in range(n_pairs):
            state_ref[p] = state[p]
        return 0

    lax.fori_loop(0, n_chunks // per_iter, chunks, 0)


def _retention(proj3):
    b, s, _ = proj3.shape
    decay, qdec, kdec, gchunk = _retention_tables()
    c = RET_CHUNK
    return pl.pallas_call(
        functools.partial(_retention_kernel, gchunk=gchunk),
        out_shape=jax.ShapeDtypeStruct((b, s, RET_V_W), BF16),
        grid=(b,),
        in_specs=[
            pl.BlockSpec((None, s, RET_Q_W), lambda i: (i, 0, _RQ_BLK256)),
            pl.BlockSpec((None, s, RET_Q_W), lambda i: (i, 0, _RK_BLK256)),
            pl.BlockSpec((None, s, RET_V_W), lambda i: (i, 0, _RV_BLK512)),
            _resident((RET_HEADS, c, c)),
            _resident((c, RET_Q_W)),
            _resident((c, RET_Q_W)),
        ],
        out_specs=pl.BlockSpec((None, s, RET_V_W), lambda i: (i, 0, 0)),
        scratch_shapes=[pltpu.VMEM((RET_HEADS // 2, LANES, 2 * RET_V_DIM), F32)],
        compiler_params=pltpu.CompilerParams(
            dimension_semantics=("arbitrary",), vmem_limit_bytes=VMEM_LIMIT),
        name="retention",
    )(proj3, proj3, proj3, jnp.asarray(decay), jnp.asarray(qdec), jnp.asarray(kdec))


ATTN_AUG_ROWS = 16
ATTN_RING = 4
ATTN_LOOKAHEAD = 2
_BIAS_TERMS = 3


def _bf16_pieces(x, n):
    out = []
    for _ in range(n):
        p = float(np.float32(x).astype(jnp.bfloat16))
        out.append(p)
        x -= p
    return out


def _steps_per_iter(n):
    return next(k for k in (14, 4, 2, 1) if n % k == 0)


def _diffattn_kernel(off_ref, lq1_ref, lk1_ref, lq2_ref, lk2_ref, q_ref, k_ref, v_ref,
                     o_ref, kaug_ref, qqt_ref, vt_ref, mask_ref, ring_ref, m_ref, acc_ref, *, lam_init, n_off):
    t = ATTN_TILE
    s_len = q_ref.shape[0]
    n_q = s_len // t
    d = DIFF_V_DIM
    lam = (jnp.exp(jnp.sum(lq1_ref[...] * lk1_ref[...], axis=-1, keepdims=True))
           - jnp.exp(jnp.sum(lq2_ref[...] * lk2_ref[...], axis=-1, keepdims=True)) + lam_init)
    head_cols = [slice(h * LANES, (h + 1) * LANES) for h in range(DIFF_HEADS)]

    @pl.when(pl.program_id(0) == 0)
    def _():
        key = lax.broadcasted_iota(jnp.int32, (t, 2 * t), 0)
        qry = lax.broadcasted_iota(jnp.int32, (t, 2 * t), 1)
        causal = jnp.where(qry >= t, qry - t, qry) >= key
        mask_ref[...] = jnp.where(causal, 0.0, -jnp.inf)
        pos = lax.broadcasted_iota(jnp.int32, (s_len, LANES), 0)
        lane = lax.broadcasted_iota(jnp.int32, (s_len, LANES), 1)
        part = jnp.where(lane % 2 == 0, pos - pos % LANES, pos % LANES)
        part = jnp.where(lane < 2 * _BIAS_TERMS, part, 0).astype(F32).astype(BF16)
        row = lax.broadcasted_iota(jnp.int32, (LANES, 2 * t), 0)
        for h in range(DIFF_HEADS):
            kaug_ref[h, :, LANES:] = part
            slope = 2.0 ** (-8.0 * (h + 1) / DIFF_HEADS)
            coef = jnp.zeros((LANES, 2 * t), F32)
            for i, piece in enumerate(_bf16_pieces(LOG2E, _BIAS_TERMS)):
                coef = jnp.where(row // 2 == i, slope * piece, coef)
            for qb in range(n_q):
                qqt_ref[h, qb, LANES:, :] = coef.astype(BF16)
        vt_ref[:, :, d:, :] = jnp.ones((DIFF_HEADS, n_q, ATTN_AUG_ROWS, t), BF16)

    sub = lax.broadcasted_iota(jnp.int32, (LANES, t), 0)
    upper = sub < DIFF_HEAD_DIM

    def stage(jb, _):
        rows = pl.ds(pl.multiple_of(jb * t, t), t)
        for h in range(DIFF_HEADS):
            kaug_ref[h, rows, :LANES] = k_ref[rows, head_cols[h]]
            vt_ref[h, jb, :d, :] = v_ref[rows, head_cols[h]].T
            qt = q_ref[rows, head_cols[h]].T
            zero = jnp.zeros_like(qt)
            qqt_ref[h, jb, :LANES, :t] = jnp.where(upper, qt, zero)
            qqt_ref[h, jb, :LANES, t:] = jnp.where(upper, zero, qt)
            m_ref[h, jb] = jnp.full(m_ref.shape[2:], -jnp.inf, F32)
            acc_ref[h, jb] = jnp.zeros(acc_ref.shape[2:], F32)
        return 0

    lax.fori_loop(0, n_q, stage, 0)

    def run_steps(n_steps, pair_of, masked, final):
        per_iter = _steps_per_iter(n_steps)
        items = per_iter * DIFF_HEADS

        def scores(step, h, slot):
            qi, j = pair_of(jnp.minimum(step, n_steps - 1))
            s = _dot(kaug_ref[h, pl.ds(pl.multiple_of(j * t, t), t), :], qqt_ref[h, qi])
            if masked:
                s = s + mask_ref[...]
            ring_ref[slot] = s
            return jnp.max(s, axis=0, keepdims=True)

        def update(step, h, slot, s_max):
            qi, j = pair_of(step)
            s = ring_ref[slot]
            m_old = m_ref[h, qi]
            m_new = jnp.maximum(m_old, s_max)
            alpha = jnp.exp2(m_old - m_new)
            p = jnp.exp2(s - m_new).astype(BF16)
            m_ref[h, qi] = m_new
            acc = alpha * acc_ref[h, qi] + _dot(vt_ref[h, j], p)
            acc_ref[h, qi] = acc
            if final:
                finish(qi, h)

        def body(it, pending):
            pending = list(pending)
            base = it * per_iter
            for idx in range(items):
                ahead = idx + ATTN_LOOKAHEAD
                new = scores(base + ahead // DIFF_HEADS, ahead % DIFF_HEADS, ahead % ATTN_RING)
                update(base + idx // DIFF_HEADS, idx % DIFF_HEADS, idx % ATTN_RING, pending[0])
                pending = pending[1:] + [new]
            return tuple(pending)

        assert items % ATTN_RING == 0 and ATTN_LOOKAHEAD < ATTN_RING
        first = tuple(scores(i // DIFF_HEADS, i % DIFF_HEADS, i) for i in range(ATTN_LOOKAHEAD))
        lax.fori_loop(0, n_steps // per_iter, body, first)

    def finish(qb, h):
        inv = 1.0 / acc_ref[h, qb, d:d + 1, :]
        o = acc_ref[h, qb, :d, :t] * inv[:, :t] - acc_ref[h, qb, :d, t:] * (lam * inv[:, t:])
        o_ref[pl.ds(pl.multiple_of(qb * t, t), t), head_cols[h]] = o.T.astype(BF16)

    if n_off:
        run_steps(n_off, lambda step: (off_ref[0, step], off_ref[1, step]), masked=False, final=False)
    run_steps(n_q, lambda step: (step, step), masked=True, final=True)


def _diffattn(proj3, lq1, lk1, lq2, lk2, lam_init):
    b, s, _ = proj3.shape
    t = ATTN_TILE
    n_q = s // t
    off = np.array([(qi, j) for qi in range(n_q) for j in range(qi)], np.int32).reshape(-1, 2).T
    n_off = off.shape[1]
    if n_off == 0:
        off = np.zeros((2, 1), np.int32)
    vec = lambda a: a.reshape(1, DIFF_HEAD_DIM).astype(F32)
    rows_aug = DIFF_V_DIM + ATTN_AUG_ROWS
    return pl.pallas_call(
        functools.partial(_diffattn_kernel, lam_init=lam_init, n_off=n_off),
        out_shape=jax.ShapeDtypeStruct((b, s, DIFF_V_W), BF16),
        grid=(b,),
        in_specs=[
            pl.BlockSpec(memory_space=pltpu.SMEM),
            _resident((1, DIFF_HEAD_DIM)), _resident((1, DIFF_HEAD_DIM)),
            _resident((1, DIFF_HEAD_DIM)), _resident((1, DIFF_HEAD_DIM)),
            pl.BlockSpec((None, s, DIFF_QK_W), lambda i: (i, 0, _DQ_BLK512)),
            pl.BlockSpec((None, s, DIFF_QK_W), lambda i: (i, 0, _DK_BLK512)),
            pl.BlockSpec((None, s, DIFF_V_W), lambda i: (i, 0, _DV_BLK512)),
        ],
        out_specs=pl.BlockSpec((None, s, DIFF_V_W), lambda i: (i, 0, 0)),
        scratch_shapes=[
            pltpu.VMEM((DIFF_HEADS, s, 2 * LANES), BF16),
            pltpu.VMEM((DIFF_HEADS, n_q, 2 * LANES, 2 * t), BF16),
            pltpu.VMEM((DIFF_HEADS, n_q, rows_aug, t), BF16),
            pltpu.VMEM((t, 2 * t), F32),
            pltpu.VMEM((ATTN_RING, t, 2 * t), F32),
            pltpu.VMEM((DIFF_HEADS, n_q, 1, 2 * t), F32),
            pltpu.VMEM((DIFF_HEADS, n_q, rows_aug, 2 * t), F32),
        ],
        compiler_params=pltpu.CompilerParams(
            dimension_semantics=("arbitrary",), vmem_limit_bytes=VMEM_LIMIT),
        name="diffattn",
    )(jnp.asarray(off), vec(lq1), vec(lk1), vec(lq2), vec(lk2), proj3, proj3, proj3)


def _silu(x):
    return x * (1.0 / (1.0 + jnp.exp(-x)))


def _mix_ffn_kernel(h_ref, ret_ref, gate_ref, dif_ref, rn_ref, dn_ref, wo_ref, fn_ref, wg_ref, wu_ref,
                    wd_ref, *rest, final, dif_scale):
    if final:
        final_ref, o_ref, a_ref = rest
    else:
        o_ref, a_ref = rest
    sub = ROW_TILE // ROW_SUBTILES
    tiles = [slice(i * sub, (i + 1) * sub) for i in range(ROW_SUBTILES)]
    dn = dn_ref[...] * dif_scale
    h1 = []
    for r in tiles:
        dif = [_rms(dif_ref[r, hd * DIFF_V_DIM:(hd + 1) * DIFF_V_DIM].astype(F32), dn).astype(BF16)
               for hd in range(DIFF_HEADS)]
        x = h_ref[r, :] + _dot(jnp.concatenate(dif, axis=-1), wo_ref[RET_V_W:, :])
        heads = []
        for hd in range(RET_HEADS):
            cols = slice(hd * RET_V_DIM, (hd + 1) * RET_V_DIM)
            ret = _rms(ret_ref[r, cols].astype(F32), rn_ref[...])
            heads.append((_silu(gate_ref[r, cols].astype(F32)) * ret).astype(BF16))
        h1.append(x + _dot(jnp.concatenate(heads, axis=-1), wo_ref[:RET_V_W, :]))
    u = [_rms(x, fn_ref[...]).astype(BF16) for x in h1]
    for i, r in enumerate(tiles):
        for c0 in range(0, FFN_HIDDEN, FFN_CHUNK):
            c1 = min(c0 + FFN_CHUNK, FFN_HIDDEN)
            up = _dot(u[i], wu_ref[:, c0:c1])
            a_ref[r, c0:c1] = (_silu(_dot(u[i], wg_ref[:, c0:c1])) * up).astype(BF16)
        out = h1[i] + _dot(a_ref[r, :], wd_ref[...])
        if final:
            out = _rms(out, final_ref[...])
        o_ref[r, :] = out


def _mix_ffn(h2, ret2, proj, dif2, rn, dn, dif_scale, wo, fn, wg, wu, wd, final_norm):
    m = h2.shape[0]
    final = final_norm is not None
    row = lambda w, blk=0: pl.BlockSpec((ROW_TILE, w), lambda i: (i, blk))
    in_specs = [
        row(D_MODEL), row(RET_V_W), row(RET_V_W, _RG_BLK512), row(DIFF_V_W),
        _resident((1, RET_V_DIM)),
        _resident((1, DIFF_V_DIM)),
        _resident((MIX_WIDTH, D_MODEL)),
        _resident((1, D_MODEL)),
        _resident((D_MODEL, FFN_HIDDEN)),
        _resident((D_MODEL, FFN_HIDDEN)),
        _resident((FFN_HIDDEN, D_MODEL)),
    ]
    args = [h2, ret2, proj, dif2, rn.reshape(1, RET_V_DIM), dn.reshape(1, DIFF_V_DIM), wo,
            fn.reshape(1, D_MODEL), wg, wu, wd]
    if final:
        in_specs.append(_resident((1, D_MODEL)))
        args.append(final_norm.reshape(1, D_MODEL))
    return pl.pallas_call(
        functools.partial(_mix_ffn_kernel, final=final, dif_scale=dif_scale),
        out_shape=jax.ShapeDtypeStruct((m, D_MODEL), F32),
        grid=(m // ROW_TILE,),
        in_specs=in_specs,
        out_specs=row(D_MODEL),
        scratch_shapes=[pltpu.VMEM((ROW_TILE, FFN_HIDDEN), BF16)],
        compiler_params=pltpu.CompilerParams(
            dimension_semantics=("arbitrary",), vmem_limit_bytes=VMEM_LIMIT),
        name="mix_ffn_final" if final else "mix_ffn",
    )(*args)


def kernel(x, attn_norm, w_in, ret_norm, lambda_q1, lambda_k1, lambda_q2, lambda_k2,
           diff_norm, w_out, ffn_norm, w_gate, w_up, w_down, final_norm):
    b, s, _ = x.shape
    depth = w_in.shape[0]
    assert (b * s) % ROW_TILE == 0 and s % ATTN_TILE == 0 and s % RET_CHUNK == 0
    h = x.reshape(b * s, D_MODEL)
    for l in range(depth):
        proj = _inproj(h, attn_norm[l], w_in[l].astype(BF16))
        proj3 = proj.reshape(b, s, IN_WIDTH)
        ret = _retention(proj3)
        lam_init = 0.8 - 0.6 * math.exp(-0.3 * l)
        dif = _diffattn(proj3, lambda_q1[l], lambda_k1[l], lambda_q2[l], lambda_k2[l], lam_init)
        h = _mix_ffn(h, ret.reshape(b * s, RET_V_W), proj, dif.reshape(b * s, DIFF_V_W), ret_norm[l],
                     diff_norm[l], 1.0 - lam_init, w_out[l].astype(BF16), ffn_norm[l], w_gate[l].astype(BF16),
                     w_up[l].astype(BF16), w_down[l].astype(BF16),
                     final_norm if l == depth - 1 else None)
    return h.reshape(b, s, D_MODEL)
```

```python
import functools
import math

import numpy as np
import jax
import jax.numpy as jnp
from jax import lax
from jax.experimental import pallas as pl
from jax.experimental.pallas import tpu as pltpu

D_MODEL = 1024
RET_HEADS = 4
RET_QK_DIM = 64
RET_V_DIM = 128
RET_CHUNK = 128
DIFF_HEADS = 4
DIFF_HEAD_DIM = 64
DIFF_V_DIM = 2 * DIFF_HEAD_DIM
FFN_HIDDEN = 2816
NORM_EPS = 1e-6

RET_Q_W = RET_HEADS * RET_QK_DIM
RET_V_W = RET_HEADS * RET_V_DIM
DIFF_QK_W = DIFF_HEADS * 2 * DIFF_HEAD_DIM
DIFF_V_W = DIFF_HEADS * DIFF_V_DIM
IN_WIDTH = 2 * RET_Q_W + 2 * RET_V_W + 2 * DIFF_QK_W + DIFF_V_W
MIX_WIDTH = RET_V_W + DIFF_V_W

LANES = 128
_RQ_BLK256 = 0
_RK_BLK256 = 1
_RV_BLK512 = 1
_RG_BLK512 = 2
_DQ_BLK512 = 3
_DK_BLK512 = 4
_DV_BLK512 = 5

ROW_TILE = 1024
ROW_SUBTILES = 4
ATTN_TILE = 256
RET_CHUNKS_PER_ITER = 8
FFN_CHUNK = 512
VMEM_LIMIT = 56 * 1024 * 1024

LOG2E = math.log2(math.e)
DQ_SCALE = DIFF_HEAD_DIM ** -0.5 * LOG2E

F32 = jnp.float32
BF16 = jnp.bfloat16


def _rms(x, g):
    return x * lax.rsqrt(jnp.mean(x * x, axis=-1, keepdims=True) + NORM_EPS) * g


def _dot(a, b):
    return jnp.dot(a, b, preferred_element_type=F32)


def _dot_nt(a, b):
    return lax.dot_general(a, b, (((1,), (1,)), ((), ())), preferred_element_type=F32)


def _dot_tn(a, b):
    return lax.dot_general(a, b, (((0,), (0,)), ((), ())), preferred_element_type=F32)


def _resident(shape):
    return pl.BlockSpec(shape, lambda *_: (0,) * len(shape), pipeline_mode=pl.Buffered(1))


def _resident_layer(shape, layer):
    return pl.BlockSpec((None,) + shape, lambda *_: (layer,) + (0,) * len(shape),
                        pipeline_mode=pl.Buffered(1))


def _inproj_kernel(x_ref, g_ref, w_ref, o_ref):
    sub = ROW_TILE // ROW_SUBTILES
    u = [_rms(x_ref[i * sub:(i + 1) * sub, :], g_ref[...]).astype(BF16) for i in range(ROW_SUBTILES)]
    step = DIFF_QK_W
    for i in range(ROW_SUBTILES):
        rows = slice(i * sub, (i + 1) * sub)
        for blk in range(IN_WIDTH // step):
            cols = slice(blk * step, (blk + 1) * step)
            y = _dot(u[i], w_ref[:, cols])
            if blk == _DQ_BLK512:
                y = y * DQ_SCALE
            o_ref[rows, cols] = y.astype(BF16)


def _inproj(x2, gain, w, layer):
    m = x2.shape[0]
    return pl.pallas_call(
        _inproj_kernel,
        out_shape=jax.ShapeDtypeStruct((m, IN_WIDTH), BF16),
        grid=(m // ROW_TILE,),
        in_specs=[
            pl.BlockSpec((ROW_TILE, D_MODEL), lambda i: (i, 0)),
            _resident((1, D_MODEL)),
            _resident_layer((D_MODEL, IN_WIDTH), layer),
        ],
        out_specs=pl.BlockSpec((ROW_TILE, IN_WIDTH), lambda i: (i, 0)),
        compiler_params=pltpu.CompilerParams(
            dimension_semantics=("arbitrary",), vmem_limit_bytes=VMEM_LIMIT),
        name="inproj",
    )(x2, gain.reshape(1, D_MODEL), w)


def _retention_tables():
    c = RET_CHUNK
    gam = 1.0 - np.exp2(-5.0 - np.arange(RET_HEADS, dtype=np.float64))
    log_g = np.log(gam)
    idx = np.arange(c, dtype=np.float64)
    rel = idx[:, None] - idx[None, :]
    decay = np.where(rel[None] >= 0, np.exp(np.maximum(rel, 0.0)[None] * log_g[:, None, None]), 0.0)
    qdec = np.exp((idx + 1.0)[:, None] * log_g[None, :])
    kdec = np.exp((c - 1 - idx)[:, None] * log_g[None, :])
    qdec = np.repeat(qdec, RET_QK_DIM, axis=1)
    kdec = np.repeat(kdec, RET_QK_DIM, axis=1)
    gchunk = np.exp(c * log_g)
    return (decay.astype(np.float32), qdec.astype(np.float32), kdec.astype(np.float32),
            tuple(float(g) for g in gchunk))


def _retention_kernel(q_ref, k_ref, v_ref, decay_ref, qdec_ref, kdec_ref, o_ref, state_ref, *, gchunk):
    c = RET_CHUNK
    n_chunks = q_ref.shape[0] // c
    n_pairs = RET_HEADS // 2
    state_ref[...] = jnp.zeros_like(state_ref)
    lane = lax.broadcasted_iota(jnp.int32, (c, LANES), 1)
    low_half = lane < RET_QK_DIM
    col = lax.broadcasted_iota(jnp.int32, (1, 2 * RET_V_DIM), 1)
    pair_decay = [jnp.where(col < RET_V_DIM, gchunk[2 * p], gchunk[2 * p + 1]) for p in range(n_pairs)]

    per_iter = RET_CHUNKS_PER_ITER if n_chunks % RET_CHUNKS_PER_ITER == 0 else 1

    def chunks(it, _):
        rows = [pl.ds(pl.multiple_of((it * per_iter + i) * c, c), c) for i in range(per_iter)]
        scores, updates, qd_all = [], [], []
        for r in rows:
            q_all = q_ref[r, :]
            k_all = k_ref[r, :] * jnp.asarray(RET_QK_DIM ** -0.5, BF16)
            qd_all.append((q_all.astype(F32) * qdec_ref[...]).astype(BF16))
            kd_all = (k_all.astype(F32) * kdec_ref[...]).astype(BF16)
            for p in range(n_pairs):
                cols = slice(p * LANES, (p + 1) * LANES)
                qp = q_all[:, cols]
                zero = jnp.zeros_like(qp)
                qm2 = jnp.concatenate([jnp.where(low_half, qp, zero), jnp.where(low_half, zero, qp)], axis=0)
                scores.append(_dot_nt(qm2, k_all[:, cols]))
            for p in range(n_pairs):
                cols = slice(p * LANES, (p + 1) * LANES)
                vcols = slice(p * 2 * RET_V_DIM, (p + 1) * 2 * RET_V_DIM)
                updates.append(_dot_tn(kd_all[:, cols], v_ref[r, vcols]))
        state = [state_ref[p] for p in range(n_pairs)]
        for i, r in enumerate(rows):
            before = [st.astype(BF16) for st in state]
            state = [pair_decay[p] * state[p] + updates[i * n_pairs + p] for p in range(n_pairs)]
            for h in range(RET_HEADS):
                p, hh = divmod(h, 2)
                cols = slice(p * LANES, (p + 1) * LANES)
                vcols = slice(h * RET_V_DIM, (h + 1) * RET_V_DIM)
                sd = (scores[i * n_pairs + p][hh * c:(hh + 1) * c] * decay_ref[h]).astype(BF16)
                qd = qd_all[i][:, cols]
                keep = low_half if hh == 0 else jnp.logical_not(low_half)
                qd = jnp.where(keep, qd, jnp.zeros_like(qd))
                lhs = jnp.concatenate([sd, qd], axis=1)
                rhs = jnp.concatenate([v_ref[r, vcols], before[p][:, hh * RET_V_DIM:(hh + 1) * RET_V_DIM]],
                                      axis=0)
                o_ref[r, vcols] = _dot(lhs, rhs).astype(BF16)
        for p in range(n_pairs):
            state_ref[p] = state[p]
        return 0

    lax.fori_loop(0, n_chunks // per_iter, chunks, 0)


def _retention(proj3):
    b, s, _ = proj3.shape
    decay, qdec, kdec, gchunk = _retention_tables()
    c = RET_CHUNK
    return pl.pallas_call(
        functools.partial(_retention_kernel, gchunk=gchunk),
        out_shape=jax.ShapeDtypeStruct((b, s, RET_V_W), BF16),
        grid=(b,),
        in_specs=[
            pl.BlockSpec((None, s, RET_Q_W), lambda i: (i, 0, _RQ_BLK256)),
            pl.BlockSpec((None, s, RET_Q_W), lambda i: (i, 0, _RK_BLK256)),
            pl.BlockSpec((None, s, RET_V_W), lambda i: (i, 0, _RV_BLK512)),
            _resident((RET_HEADS, c, c)),
            _resident((c, RET_Q_W)),
            _resident((c, RET_Q_W)),
        ],
        out_specs=pl.BlockSpec((None, s, RET_V_W), lambda i: (i, 0, 0)),
        scratch_shapes=[pltpu.VMEM((RET_HEADS // 2, LANES, 2 * RET_V_DIM), F32)],
        compiler_params=pltpu.CompilerParams(
            dimension_semantics=("arbitrary",), vmem_limit_bytes=VMEM_LIMIT),
        name="retention",
    )(proj3, proj3, proj3, jnp.asarray(decay), jnp.asarray(qdec), jnp.asarray(kdec))


ATTN_AUG_ROWS = 16
ATTN_RING = 4
ATTN_LOOKAHEAD = 2
_BIAS_TERMS = 3


def _bf16_pieces(x, n):
    out = []
    for _ in range(n):
        p = float(np.float32(x).astype(jnp.bfloat16))
        out.append(p)
        x -= p
    return out


def _steps_per_iter(n):
    return next(k for k in (14, 4, 2, 1) if n % k == 0)


def _diffattn_kernel(off_ref, lq1_ref, lk1_ref, lq2_ref, lk2_ref, q_ref, k_ref, v_ref,
                     o_ref, kaug_ref, qqt_ref, vt_ref, mask_ref, ring_ref, m_ref, acc_ref, *, lam_init, n_off):
    t = ATTN_TILE
    s_len = q_ref.shape[0]
    n_q = s_len // t
    d = DIFF_V_DIM
    lam = (jnp.exp(jnp.sum(lq1_ref[...] * lk1_ref[...], axis=-1, keepdims=True))
           - jnp.exp(jnp.sum(lq2_ref[...] * lk2_ref[...], axis=-1, keepdims=True)) + lam_init)
    head_cols = [slice(h * LANES, (h + 1) * LANES) for h in range(DIFF_HEADS)]

    @pl.when(pl.program_id(0) == 0)
    def _():
        key = lax.broadcasted_iota(jnp.int32, (t, 2 * t), 0)
        qry = lax.broadcasted_iota(jnp.int32, (t, 2 * t), 1)
        causal = jnp.where(qry >= t, qry - t, qry) >= key
        mask_ref[...] = jnp.where(causal, 0.0, -jnp.inf)
        pos = lax.broadcasted_iota(jnp.int32, (s_len, LANES), 0)
        lane = lax.broadcasted_iota(jnp.int32, (s_len, LANES), 1)
        part = jnp.where(lane % 2 == 0, pos - pos % LANES, pos % LANES)
        part = jnp.where(lane < 2 * _BIAS_TERMS, part, 0).astype(F32).astype(BF16)
        row = lax.broadcasted_iota(jnp.int32, (LANES, 2 * t), 0)
        for h in range(DIFF_HEADS):
            kaug_ref[h, :, LANES:] = part
            slope = 2.0 ** (-8.0 * (h + 1) / DIFF_HEADS)
            coef = jnp.zeros((LANES, 2 * t), F32)
            for i, piece in enumerate(_bf16_pieces(LOG2E, _BIAS_TERMS)):
                coef = jnp.where(row // 2 == i, slope * piece, coef)
            for qb in range(n_q):
                qqt_ref[h, qb, LANES:, :] = coef.astype(BF16)
        vt_ref[:, :, d:, :] = jnp.ones((DIFF_HEADS, n_q, ATTN_AUG_ROWS, t), BF16)

    sub = lax.broadcasted_iota(jnp.int32, (LANES, t), 0)
    upper = sub < DIFF_HEAD_DIM

    def stage(jb, _):
        rows = pl.ds(pl.multiple_of(jb * t, t), t)
        for h in range(DIFF_HEADS):
            kaug_ref[h, rows, :LANES] = k_ref[rows, head_cols[h]]
            vt_ref[h, jb, :d, :] = v_ref[rows, head_cols[h]].T
            qt = q_ref[rows, head_cols[h]].T
            zero = jnp.zeros_like(qt)
            qqt_ref[h, jb, :LANES, :t] = jnp.where(upper, qt, zero)
            qqt_ref[h, jb, :LANES, t:] = jnp.where(upper, zero, qt)
            m_ref[h, jb] = jnp.full(m_ref.shape[2:], -jnp.inf, F32)
            acc_ref[h, jb] = jnp.zeros(acc_ref.shape[2:], F32)
        return 0

    lax.fori_loop(0, n_q, stage, 0)

    def run_steps(n_steps, pair_of, masked, final):
        per_iter = _steps_per_iter(n_steps)
        items = per_iter * DIFF_HEADS

        def scores(step, h, slot):
            qi, j = pair_of(jnp.minimum(step, n_steps - 1))
            s = _dot(kaug_ref[h, pl.ds(pl.multiple_of(j * t, t), t), :], qqt_ref[h, qi])
            if masked:
                s = s + mask_ref[...]
            ring_ref[slot] = s
            return jnp.max(s, axis=0, keepdims=True)

        def update(step, h, slot, s_max):
            qi, j = pair_of(step)
            s = ring_ref[slot]
            m_old = m_ref[h, qi]
            m_new = jnp.maximum(m_old, s_max)
            alpha = jnp.exp2(m_old - m_new)
            p = jnp.exp2(s - m_new).astype(BF16)
            m_ref[h, qi] = m_new
            acc = alpha * acc_ref[h, qi] + _dot(vt_ref[h, j], p)
            acc_ref[h, qi] = acc
            if final:
                finish(qi, h)

        def body(it, pending):
            pending = list(pending)
            base = it * per_iter
            for idx in range(items):
                ahead = idx + ATTN_LOOKAHEAD
                new = scores(base + ahead // DIFF_HEADS, ahead % DIFF_HEADS, ahead % ATTN_RING)
                update(base + idx // DIFF_HEADS, idx % DIFF_HEADS, idx % ATTN_RING, pending[0])
                pending = pending[1:] + [new]
            return tuple(pending)

        assert items % ATTN_RING == 0 and ATTN_LOOKAHEAD < ATTN_RING
        first = tuple(scores(i // DIFF_HEADS, i % DIFF_HEADS, i) for i in range(ATTN_LOOKAHEAD))
        lax.fori_loop(0, n_steps // per_iter, body, first)

    def finish(qb, h):
        inv = 1.0 / acc_ref[h, qb, d:d + 1, :]
        o = acc_ref[h, qb, :d, :t] * inv[:, :t] - acc_ref[h, qb, :d, t:] * (lam * inv[:, t:])
        o_ref[pl.ds(pl.multiple_of(qb * t, t), t), head_cols[h]] = o.T.astype(BF16)

    if n_off:
        run_steps(n_off, lambda step: (off_ref[0, step], off_ref[1, step]), masked=False, final=False)
    run_steps(n_q, lambda step: (step, step), masked=True, final=True)


def _diffattn(proj3, lq1, lk1, lq2, lk2, lam_init):
    b, s, _ = proj3.shape
    t = ATTN_TILE
    n_q = s // t
    off = np.array([(qi, j) for qi in range(n_q) for j in range(qi)], np.int32).reshape(-1, 2).T
    n_off = off.shape[1]
    if n_off == 0:
        off = np.zeros((2, 1), np.int32)
    vec = lambda a: a.reshape(1, DIFF_HEAD_DIM).astype(F32)
    rows_aug = DIFF_V_DIM + ATTN_AUG_ROWS
    return pl.pallas_call(
        functools.partial(_diffattn_kernel, lam_init=lam_init, n_off=n_off),
        out_shape=jax.ShapeDtypeStruct((b, s, DIFF_V_W), BF16),
        grid=(b,),
        in_specs=[
            pl.BlockSpec(memory_space=pltpu.SMEM),
            _resident((1, DIFF_HEAD_DIM)), _resident((1, DIFF_HEAD_DIM)),
            _resident((1, DIFF_HEAD_DIM)), _resident((1, DIFF_HEAD_DIM)),
            pl.BlockSpec((None, s, DIFF_QK_W), lambda i: (i, 0, _DQ_BLK512)),
            pl.BlockSpec((None, s, DIFF_QK_W), lambda i: (i, 0, _DK_BLK512)),
            pl.BlockSpec((None, s, DIFF_V_W), lambda i: (i, 0, _DV_BLK512)),
        ],
        out_specs=pl.BlockSpec((None, s, DIFF_V_W), lambda i: (i, 0, 0)),
        scratch_shapes=[
            pltpu.VMEM((DIFF_HEADS, s, 2 * LANES), BF16),
            pltpu.VMEM((DIFF_HEADS, n_q, 2 * LANES, 2 * t), BF16),
            pltpu.VMEM((DIFF_HEADS, n_q, rows_aug, t), BF16),
            pltpu.VMEM((t, 2 * t), F32),
            pltpu.VMEM((ATTN_RING, t, 2 * t), F32),
            pltpu.VMEM((DIFF_HEADS, n_q, 1, 2 * t), F32),
            pltpu.VMEM((DIFF_HEADS, n_q, rows_aug, 2 * t), F32),
        ],
        compiler_params=pltpu.CompilerParams(
            dimension_semantics=("arbitrary",), vmem_limit_bytes=VMEM_LIMIT),
        name="diffattn",
    )(jnp.asarray(off), vec(lq1), vec(lk1), vec(lq2), vec(lk2), proj3, proj3, proj3)


def _silu(x):
    return x * (1.0 / (1.0 + jnp.exp(-x)))


def _mix_ffn_kernel(h_ref, ret_ref, gate_ref, dif_ref, rn_ref, dn_ref, wo_ref, fn_ref, wg_ref, wu_ref,
                    wd_ref, *rest, final, dif_scale):
    if final:
        final_ref, o_ref, a_ref = rest
    else:
        o_ref, a_ref = rest
    sub = ROW_TILE // ROW_SUBTILES
    tiles = [slice(i * sub, (i + 1) * sub) for i in range(ROW_SUBTILES)]
    dn = dn_ref[...] * dif_scale
    h1 = []
    for r in tiles:
        dif = [_rms(dif_ref[r, hd * DIFF_V_DIM:(hd + 1) * DIFF_V_DIM].astype(F32), dn).astype(BF16)
               for hd in range(DIFF_HEADS)]
        x = h_ref[r, :] + _dot(jnp.concatenate(dif, axis=-1), wo_ref[RET_V_W:, :])
        heads = []
        for hd in range(RET_HEADS):
            cols = slice(hd * RET_V_DIM, (hd + 1) * RET_V_DIM)
            ret = _rms(ret_ref[r, cols].astype(F32), rn_ref[...])
            heads.append((_silu(gate_ref[r, cols].astype(F32)) * ret).astype(BF16))
        h1.append(x + _dot(jnp.concatenate(heads, axis=-1), wo_ref[:RET_V_W, :]))
    u = [_rms(x, fn_ref[...]).astype(BF16) for x in h1]
    for i, r in enumerate(tiles):
        for c0 in range(0, FFN_HIDDEN, FFN_CHUNK):
            c1 = min(c0 + FFN_CHUNK, FFN_HIDDEN)
            up = _dot(u[i], wu_ref[:, c0:c1])
            a_ref[r, c0:c1] = (_silu(_dot(u[i], wg_ref[:, c0:c1])) * up).astype(BF16)
        out = h1[i] + _dot(a_ref[r, :], wd_ref[...])
        if final:
            out = _rms(out, final_ref[...])
        o_ref[r, :] = out


def _mix_ffn(h2, ret2, proj, dif2, rn, dn, dif_scale, wo, fn, wg, wu, wd, layer, final_norm):
    m = h2.shape[0]
    final = final_norm is not None
    row = lambda w, blk=0: pl.BlockSpec((ROW_TILE, w), lambda i: (i, blk))
    in_specs = [
        row(D_MODEL), row(RET_V_W), row(RET_V_W, _RG_BLK512), row(DIFF_V_W),
        _resident((1, RET_V_DIM)),
        _resident((1, DIFF_V_DIM)),
        _resident_layer((MIX_WIDTH, D_MODEL), layer),
        _resident((1, D_MODEL)),
        _resident_layer((D_MODEL, FFN_HIDDEN), layer),
        _resident_layer((D_MODEL, FFN_HIDDEN), layer),
        _resident_layer((FFN_HIDDEN, D_MODEL), layer),
    ]
    args = [h2, ret2, proj, dif2, rn.reshape(1, RET_V_DIM), dn.reshape(1, DIFF_V_DIM), wo,
            fn.reshape(1, D_MODEL), wg, wu, wd]
    if final:
        in_specs.append(_resident((1, D_MODEL)))
        args.append(final_norm.reshape(1, D_MODEL))
    return pl.pallas_call(
        functools.partial(_mix_ffn_kernel, final=final, dif_scale=dif_scale),
        out_shape=jax.ShapeDtypeStruct((m, D_MODEL), F32),
        grid=(m // ROW_TILE,),
        in_specs=in_specs,
        out_specs=row(D_MODEL),
        scratch_shapes=[pltpu.VMEM((ROW_TILE, FFN_HIDDEN), BF16)],
        compiler_params=pltpu.CompilerParams(
            dimension_semantics=("arbitrary",), vmem_limit_bytes=VMEM_LIMIT),
        name="mix_ffn_final" if final else "mix_ffn",
    )(*args)


def kernel(x, attn_norm, w_in, ret_norm, lambda_q1, lambda_k1, lambda_q2, lambda_k2,
           diff_norm, w_out, ffn_norm, w_gate, w_up, w_down, final_norm):
    b, s, _ = x.shape
    depth = w_in.shape[0]
    assert (b * s) % ROW_TILE == 0 and s % ATTN_TILE == 0 and s % RET_CHUNK == 0
    h = x.reshape(b * s, D_MODEL)
    w_in, w_out, w_gate, w_up, w_down = (w.astype(BF16) for w in (w_in, w_out, w_gate, w_up, w_down))
    for l in range(depth):
        proj = _inproj(h, attn_norm[l], w_in, l)
        proj3 = proj.reshape(b, s, IN_WIDTH)
        ret = _retention(proj3)
        lam_init = 0.8 - 0.6 * math.exp(-0.3 * l)
        dif = _diffattn(proj3, lambda_q1[l], lambda_k1[l], lambda_q2[l], lambda_k2[l], lam_init)
        h = _mix_ffn(h, ret.reshape(b * s, RET_V_W), proj, dif.reshape(b * s, DIFF_V_W), ret_norm[l],
                     diff_norm[l], 1.0 - lam_init, w_out, ffn_norm[l], w_gate, w_up, w_down, l,
                     final_norm if l == depth - 1 else None)
    return h.reshape(b, s, D_MODEL)
```
